```python
import math
import jax, jax.numpy as jnp
from jax import lax
import numpy as np

D_MODEL = 1024
BATCH = 8
SEQ = 2048
DEPTH = 2

CHUNK = 64
A_HEADS = 8
A_KV_HEADS = 2
A_HEAD_DIM = 64
WINDOW = 128
WINDOW_CHUNKS = WINDOW // CHUNK
B_HEADS = 8
B_Q_LORA = 384
B_KV_LORA = 256
B_NOPE = 64
B_ROPE = 32
B_V = 64
ROPE_THETA = 10000.0
Q_BLOCK = 128
C_HEADS = 4
C_KEY_DIM = 128
C_VAL_DIM = 128
D_CH = 512
CONV_W = 3
D_FF = 3584
N_EXPERTS = 8
TOP_K = 2
EPS = 1e-6
NEG_INF = -1e30

A_Q = A_HEADS * A_HEAD_DIM
A_KV = A_KV_HEADS * A_HEAD_DIM
EVEN_SIZES = (A_Q, A_KV, A_KV, B_Q_LORA, B_KV_LORA, B_ROPE)
EVEN_IN = A_Q + 2 * A_KV + B_Q_LORA + B_KV_LORA + B_ROPE
EVEN_MIX = A_Q + B_HEADS * B_V
C_QK = C_HEADS * C_KEY_DIM
C_WIDTH = C_HEADS * C_VAL_DIM
ODD_SIZES = (C_QK, C_QK, C_WIDTH, C_WIDTH, D_CH, D_CH, D_CH)
ODD_IN = 2 * C_QK + 2 * C_WIDTH + 3 * D_CH
ODD_MIX = C_WIDTH + D_CH
N_EVEN = (DEPTH + 1) // 2
N_ODD = DEPTH // 2

kernel_name = "chunk_causal_hybrid_swa_mla_retnet_shortconv_moe"


def _split(t, sizes):
    outs, start = [], 0
    for s in sizes:
        outs.append(t[..., start:start + s])
        start += s
    return outs


def rmsnorm(x, g):
    xf = x.astype(jnp.float32)
    y = xf * lax.rsqrt(jnp.mean(xf * xf, axis=-1, keepdims=True) + EPS)
    return (y * g.astype(jnp.float32)).astype(x.dtype)


def swiglu(h, w_gate, w_up, w_down):
    return (jax.nn.silu(h @ w_gate) * (h @ w_up)) @ w_down


def rope(t, cos, sin):
    half = t.shape[-1] // 2
    t1, t2 = t[..., :half], t[..., half:]
    return jnp.concatenate([t1 * cos - t2 * sin, t2 * cos + t1 * sin], axis=-1).astype(t.dtype)


def sliding_window_gqa(q, k, v, sinks, slopes):
    Bn, S, Hq, d = q.shape
    Hkv = k.shape[2]
    G = Hq // Hkv
    N = S // CHUNK
    band = (WINDOW_CHUNKS + 1) * CHUNK
    pad = ((0, 0), (WINDOW_CHUNKS * CHUNK, 0), (0, 0), (0, 0))
    kp, vp = jnp.pad(k, pad), jnp.pad(v, pad)

    def banded(t):
        parts = [t[:, j * CHUNK:j * CHUNK + S].reshape(Bn, N, CHUNK, Hkv, d)
                 for j in range(WINDOW_CHUNKS + 1)]
        return jnp.concatenate(parts, axis=2)

    kb, vb = banded(kp), banded(vp)
    qc = q.reshape(Bn, N, CHUNK, Hkv, G, d)
    s = jnp.einsum('bnqhgd,bnkhd->bnhgqk', qc, kb).astype(jnp.float32) * d ** -0.5
    qi = jnp.arange(CHUNK)
    kk = jnp.arange(band)
    dist = jnp.abs(qi[:, None] + WINDOW_CHUNKS * CHUNK - kk[None, :]).astype(jnp.float32)
    bias = -slopes.reshape(Hkv, G)[:, :, None, None] * dist
    key_pos = jnp.arange(N)[:, None] * CHUNK - WINDOW_CHUNKS * CHUNK + kk[None, :]
    valid = (key_pos >= 0)[None, :, None, None, None, :]
    s = jnp.where(valid, s + bias, NEG_INF)
    sink = jnp.broadcast_to(sinks.astype(jnp.float32).reshape(Hkv, G)[:, :, None, None],
                            s.shape[:-1] + (1,))
    p = jax.nn.softmax(jnp.concatenate([s, sink], axis=-1), axis=-1)[..., :-1]
    o = jnp.einsum('bnhgqk,bnkhd->bnqhgd', p.astype(v.dtype), vb)
    return o.reshape(Bn, S, Hq * d)


def mla(c_q, c_kv, k_rope_raw, q_norm, w_uq, kv_norm, w_ukv, cos, sin):
    Bn, S, _ = c_q.shape
    q = (rmsnorm(c_q, q_norm) @ w_uq).reshape(Bn, S, B_HEADS, B_NOPE + B_ROPE)
    q_nope = q[..., :B_NOPE]
    q_rope = rope(q[..., B_NOPE:], cos[:, :, None], sin[:, :, None])
    kv = (rmsnorm(c_kv, kv_norm) @ w_ukv).reshape(Bn, S, B_HEADS, B_NOPE + B_V)
    k_nope, v = kv[..., :B_NOPE], kv[..., B_NOPE:]
    k_rope = rope(k_rope_raw, cos, sin)
    NB = S // Q_BLOCK
    scale = (B_NOPE + B_ROPE) ** -0.5
    key_chunk = jnp.arange(S) // CHUNK

    def to_blocks(t):
        return t.reshape(Bn, NB, Q_BLOCK, *t.shape[2:]).swapaxes(0, 1)

    def block(args):
        qn, qr, bi = args
        s = (jnp.einsum('bqhd,bkhd->bhqk', qn, k_nope)
             + jnp.einsum('bqhd,bkd->bhqk', qr, k_rope)).astype(jnp.float32) * scale
        q_chunk = (bi * Q_BLOCK + jnp.arange(Q_BLOCK)) // CHUNK
        s = jnp.where(key_chunk[None, :] <= q_chunk[:, None], s, NEG_INF)
        p = jax.nn.softmax(s, axis=-1)
        return jnp.einsum('bhqk,bkhd->bqhd', p.astype(v.dtype), v)

    o = lax.map(block, (to_blocks(q_nope), to_blocks(q_rope), jnp.arange(NB)))
    return o.swapaxes(0, 1).reshape(Bn, S, B_HEADS * B_V)


def retention(q, k, v):
    Bn, S, H, dk = q.shape
    dv = v.shape[-1]
    N = S // CHUNK
    f32 = jnp.float32
    q = q.astype(f32)
    k = k.astype(f32) * dk ** -0.5
    v = v.astype(f32)

    def to_chunks(t):
        return t.reshape(Bn, N, CHUNK, H, t.shape[-1]).transpose(1, 0, 3, 2, 4)

    log_g = jnp.log(1.0 - 2.0 ** (-5.0 - jnp.arange(H, dtype=f32)))
    i = jnp.arange(CHUNK, dtype=f32)
    diff = i[:, None] - i[None, :]
    intra_decay = jnp.where(diff >= 0, jnp.exp(log_g[:, None, None] * jnp.maximum(diff, 0.0)), 0.0)
    cross_decay = jnp.exp(log_g[:, None] * (i + 1.0))
    state_decay = jnp.exp(log_g[:, None] * (CHUNK - 1.0 - i))
    chunk_decay = jnp.exp(log_g * CHUNK)

    def step(R, qkv):
        qc, kc, vc = qkv
        scores = jnp.einsum('bhqd,bhkd->bhqk', qc, kc) * intra_decay
        inner = jnp.einsum('bhqk,bhkv->bhqv', scores, vc)
        cross = jnp.einsum('bhqd,bhdv->bhqv', qc, R) * cross_decay[..., None]
        R = chunk_decay[:, None, None] * R + jnp.einsum('bhkd,bhkv->bhdv', kc * state_decay[..., None], vc)
        return R, inner + cross

    R0 = jnp.zeros((Bn, H, dk, dv), f32)
    _, out = lax.scan(step, R0, (to_chunks(q), to_chunks(k), to_chunks(v)))
    return out.transpose(1, 0, 3, 2, 4).reshape(Bn, S, H, dv)


def head_group_norm(y, g):
    mu = jnp.mean(y, axis=-1, keepdims=True)
    var = jnp.mean(jnp.square(y - mu), axis=-1, keepdims=True)
    yn = (y - mu) * lax.rsqrt(var + EPS)
    return yn.reshape(y.shape[0], y.shape[1], -1) * g.astype(jnp.float32)


def short_gated_conv(b_gate, c_gate, h, w):
    S = h.shape[1]
    u = jnp.pad(c_gate * h, ((0, 0), (CONV_W - 1, 0), (0, 0)))
    y = sum(w[j] * u[:, j:j + S] for j in range(CONV_W))
    return b_gate * y


def moe(h, router, we_gate, we_up, we_down):
    logits = jnp.einsum('bsd,de->bse', h, router).astype(jnp.float32)
    top_val, top_idx = lax.top_k(logits, TOP_K)
    gates = jax.nn.softmax(top_val, axis=-1)
    combine = jnp.sum(jax.nn.one_hot(top_idx, N_EXPERTS, dtype=jnp.float32) * gates[..., None], axis=-2)
    out = jnp.zeros(h.shape, jnp.float32)
    for e in range(N_EXPERTS):
        out = out + combine[..., e:e + 1] * swiglu(h, we_gate[e], we_up[e], we_down[e]).astype(jnp.float32)
    return out.astype(h.dtype)


def setup_inputs(seed: int = 0) -> dict:
    key = jax.random.key(seed)
    ks = iter(jax.random.split(key, 32))
    f32 = jnp.float32

    def w(shape, fan_in):
        return jax.random.normal(next(ks), shape, f32) * fan_in ** -0.5

    def gain(shape):
        return 1.0 + 0.1 * jax.random.normal(next(ks), shape, f32)

    x = jax.random.normal(next(ks), (BATCH, SEQ, D_MODEL), f32)
    offsets = jax.random.randint(next(ks), (BATCH, 1), 0, 64, dtype=jnp.int32) * CHUNK
    positions = (offsets + jnp.arange(SEQ, dtype=jnp.int32)[None, :]).astype(jnp.int32)
    return {
        "x": x,
        "positions": positions,
        "even_norm_mix": gain((N_EVEN, D_MODEL)),
        "even_w_in": w((N_EVEN, D_MODEL, EVEN_IN), D_MODEL),
        "even_sinks": jax.random.normal(next(ks), (N_EVEN, A_HEADS), f32),
        "even_q_norm": gain((N_EVEN, B_Q_LORA)),
        "even_w_uq": w((N_EVEN, B_Q_LORA, B_HEADS * (B_NOPE + B_ROPE)), B_Q_LORA),
        "even_kv_norm": gain((N_EVEN, B_KV_LORA)),
        "even_w_ukv": w((N_EVEN, B_KV_LORA, B_HEADS * (B_NOPE + B_V)), B_KV_LORA),
        "even_w_out": w((N_EVEN, EVEN_MIX, D_MODEL), EVEN_MIX),
        "even_norm_ffn": gain((N_EVEN, D_MODEL)),
        "even_w_gate": w((N_EVEN, D_MODEL, D_FF), D_MODEL),
        "even_w_up": w((N_EVEN, D_MODEL, D_FF), D_MODEL),
        "even_w_down": w((N_EVEN, D_FF, D_MODEL), D_FF),
        "odd_norm_mix": gain((N_ODD, D_MODEL)),
        "odd_w_in": w((N_ODD, D_MODEL, ODD_IN), D_MODEL),
        "odd_ret_gn": gain((N_ODD, C_WIDTH)),
        "odd_conv_w": w((N_ODD, CONV_W, D_CH), CONV_W),
        "odd_w_out": w((N_ODD, ODD_MIX, D_MODEL), ODD_MIX),
        "odd_norm_ffn": gain((N_ODD, D_MODEL)),
        "odd_router": w((N_ODD, D_MODEL, N_EXPERTS), D_MODEL),
        "odd_we_gate": w((N_ODD, N_EXPERTS, D_MODEL, D_FF), D_MODEL),
        "odd_we_up": w((N_ODD, N_EXPERTS, D_MODEL, D_FF), D_MODEL),
        "odd_we_down": w((N_ODD, N_EXPERTS, D_FF, D_MODEL), D_FF),
        "final_norm": gain((D_MODEL,)),
    }


def reference(x, positions, even_norm_mix, even_w_in, even_sinks, even_q_norm, even_w_uq,
              even_kv_norm, even_w_ukv, even_w_out, even_norm_ffn, even_w_gate, even_w_up,
              even_w_down, odd_norm_mix, odd_w_in, odd_ret_gn, odd_conv_w, odd_w_out,
              odd_norm_ffn, odd_router, odd_we_gate, odd_we_up, odd_we_down, final_norm):
    Bn, S, _ = x.shape
    inv_freq = ROPE_THETA ** (-jnp.arange(0, B_ROPE, 2, dtype=jnp.float32) / B_ROPE)
    ang = positions.astype(jnp.float32)[..., None] * inv_freq
    cos, sin = jnp.cos(ang), jnp.sin(ang)
    slopes = 2.0 ** (-8.0 * (jnp.arange(A_HEADS, dtype=jnp.float32) + 1.0) / A_HEADS)

    for layer in range(DEPTH):
        i = layer // 2
        if layer % 2 == 0:
            h = rmsnorm(x, even_norm_mix[i])
            qa, ka, va, cq, ckv, kr = _split(h @ even_w_in[i], EVEN_SIZES)
            ya = sliding_window_gqa(qa.reshape(Bn, S, A_HEADS, A_HEAD_DIM),
                                    ka.reshape(Bn, S, A_KV_HEADS, A_HEAD_DIM),
                                    va.reshape(Bn, S, A_KV_HEADS, A_HEAD_DIM),
                                    even_sinks[i], slopes)
            yb = mla(cq, ckv, kr, even_q_norm[i], even_w_uq[i], even_kv_norm[i], even_w_ukv[i], cos, sin)
            x = x + (jnp.concatenate([ya.astype(x.dtype), yb.astype(x.dtype)], axis=-1) @ even_w_out[i])
            x = x + swiglu(rmsnorm(x, even_norm_ffn[i]), even_w_gate[i], even_w_up[i], even_w_down[i])
        else:
            h = rmsnorm(x, odd_norm_mix[i])
            qc, kc, vc, gc, bd, cd, hd = _split(h @ odd_w_in[i], ODD_SIZES)
            yr = retention(qc.reshape(Bn, S, C_HEADS, C_KEY_DIM),
                           kc.reshape(Bn, S, C_HEADS, C_KEY_DIM),
                           vc.reshape(Bn, S, C_HEADS, C_VAL_DIM))
            yc = (jax.nn.silu(gc.astype(jnp.float32)) * head_group_norm(yr, odd_ret_gn[i])).astype(x.dtype)
            yd = short_gated_conv(bd, cd, hd, odd_conv_w[i]).astype(x.dtype)
            x = x + (jnp.concatenate([yc, yd], axis=-1) @ odd_w_out[i])
            x = x + moe(rmsnorm(x, odd_norm_ffn[i]), odd_router[i], odd_we_gate[i], odd_we_up[i], odd_we_down[i])
    return rmsnorm(x, final_norm)
```

```python
import functools
import math

import numpy as np
import jax
import jax.numpy as jnp
from jax import lax
from jax.experimental import pallas as pl
from jax.experimental.pallas import tpu as pltpu

F32 = jnp.float32
BF16 = jnp.bfloat16

D_MODEL = 1024
CHUNK = 64
A_HEADS = 8
A_KV_HEADS = 2
A_HEAD_DIM = 64
A_GROUP = A_HEADS // A_KV_HEADS
WINDOW_CHUNKS = 2
B_HEADS = 8
B_Q_LORA = 384
B_KV_LORA = 256
B_NOPE = 64
B_ROPE = 32
B_V = 64
ROPE_THETA = 10000.0
C_HEADS = 4
C_KEY_DIM = 128
C_VAL_DIM = 128
D_CH = 512
CONV_W = 3
D_FF = 3584
N_EXPERTS = 8
TOP_K = 2
EPS = 1e-6
NEG_INF = -1e30

LANE = 128
ROW_TILE = 512
FF_TILE = 1792
ATT_Q = 128
MLA_T = 256
GATHER_ROWS = 256
VMEM_LIMIT = 56 * 1024 * 1024

_EV_QA = A_HEADS * LANE
_EV_KA = A_KV_HEADS * LANE
_EV_VA = A_KV_HEADS * LANE
_EV_OFF = np.cumsum([0, _EV_QA, _EV_KA, _EV_VA, _EV_VA, B_Q_LORA, B_KV_LORA, LANE, LANE])
_EV_COLS = int(_EV_OFF[-1])


def _cparams(semantics):
    return pltpu.CompilerParams(dimension_semantics=semantics, vmem_limit_bytes=VMEM_LIMIT)


def _rms(xf, g):
    return xf * lax.rsqrt(jnp.mean(xf * xf, axis=-1, keepdims=True) + EPS) * g


def _dot(a, b):
    return jnp.dot(a, b, preferred_element_type=F32)


def _dot_nt(a, b):
    return lax.dot_general(a, b, (((1,), (1,)), ((), ())), preferred_element_type=F32)


def _even_in_body(x_ref, pos_ref, g_ref, win_ref, invf_ref, qn_ref, wuqc_ref, wuqr_ref,
                  kvn_ref, wukc_ref, wuv_ref,
                  qa_ref, ka_ref, valo_ref, vahi_ref, qb_ref, kb_ref, vb_ref):
    h = _rms(x_ref[...], g_ref[...]).astype(BF16)

    def proj(k):
        return _dot(h, win_ref[:, int(_EV_OFF[k]):int(_EV_OFF[k + 1])])

    qa_ref[...] = (proj(0) * (A_HEAD_DIM ** -0.5)).astype(BF16)
    ka_ref[...] = proj(1).astype(BF16)
    valo_ref[...] = proj(2).astype(BF16)
    vahi_ref[...] = proj(3).astype(BF16)

    ang = pos_ref[...].astype(F32) * invf_ref[...]
    lane = lax.broadcasted_iota(jnp.int32, ang.shape, 1)
    is_rope = (lane >= B_NOPE) & (lane < B_NOPE + B_ROPE)
    cosm = jnp.where(lane < B_NOPE, 1.0, jnp.where(is_rope, jnp.cos(ang), 0.0))
    sinm = jnp.where(is_rope, jnp.sin(ang), 0.0)

    cq = _rms(proj(4), qn_ref[...]).astype(BF16)
    q_main = _dot(cq, wuqc_ref[...])
    q_rot = _dot(cq, wuqr_ref[...])
    k_rope = proj(6) * cosm + proj(7) * sinm
    ckv = _rms(proj(5), kvn_ref[...]).astype(BF16)
    k_main = _dot(ckv, wukc_ref[...])
    scale = (B_NOPE + B_ROPE) ** -0.5
    for hd in range(B_HEADS):
        sl = slice(hd * LANE, (hd + 1) * LANE)
        qb_ref[:, sl] = ((q_main[:, sl] * cosm + q_rot[:, sl] * sinm) * scale).astype(BF16)
        kb_ref[:, sl] = (k_main[:, sl] + k_rope).astype(BF16)
    vb_ref[...] = _dot(ckv, wuv_ref[...]).astype(BF16)


def _even_in(x2, pos2, g, w_in, inv_freq_row, q_norm, wuq_cat, wuq_rot, kv_norm, wuk_cat, wuv):
    T = x2.shape[0]
    tm = ROW_TILE
    row = lambda n: pl.BlockSpec((tm, n), lambda i: (i, 0))
    full = lambda a: pl.BlockSpec(a.shape, lambda i: (0, 0))
    outs = [_EV_QA, _EV_KA, _EV_VA, _EV_VA, B_HEADS * LANE, B_HEADS * LANE, B_HEADS * LANE]
    return pl.pallas_call(
        _even_in_body,
        grid=(T // tm,),
        in_specs=[row(D_MODEL), row(1), full(g), full(w_in), full(inv_freq_row), full(q_norm),
                  full(wuq_cat), full(wuq_rot), full(kv_norm), full(wuk_cat), full(wuv)],
        out_specs=[row(n) for n in outs],
        out_shape=[jax.ShapeDtypeStruct((T, n), BF16) for n in outs],
        compiler_params=_cparams(("parallel",)),
        name="even_in_proj",
    )(x2, pos2, g, w_in, inv_freq_row, q_norm, wuq_cat, wuq_rot, kv_norm, wuk_cat, wuv)


def _swa_body(slopes_ref, sinks_ref, q_ref, k_ref, vlo_ref, vhi_ref, o_ref):
    hk = pl.program_id(1)
    S = q_ref.shape[1]
    win = ATT_Q + WINDOW_CHUNKS * CHUNK

    def qblock(i, carry):
        q0 = pl.multiple_of(i * ATT_Q, ATT_Q)
        k0 = pl.multiple_of(jnp.maximum(q0 - WINDOW_CHUNKS * CHUNK, 0), ATT_Q)
        kw = k_ref[0, pl.ds(k0, win), :]
        vw = (vlo_ref[0, pl.ds(k0, win), :], vhi_ref[0, pl.ds(k0, win), :])
        qpos = q0 + lax.broadcasted_iota(jnp.int32, (ATT_Q, win), 0)
        kpos = k0 + lax.broadcasted_iota(jnp.int32, (ATT_Q, win), 1)
        qc = qpos // CHUNK
        kc = kpos // CHUNK
        valid = (kc <= qc) & (kc >= qc - WINDOW_CHUNKS)
        dist = jnp.abs(qpos - kpos).astype(F32)
        for pair in range(A_GROUP // 2):
            o = jnp.zeros((ATT_Q, LANE), F32)
            for w in range(2):
                g = pair * 2 + w
                hq = hk * A_GROUP + g
                q = q_ref[0, pl.ds(q0, ATT_Q), g * LANE:(g + 1) * LANE]
                s = _dot_nt(q, kw)
                s = jnp.where(valid, s - slopes_ref[hq] * dist, NEG_INF)
                sink = sinks_ref[hq]
                m = jnp.maximum(jnp.max(s, axis=-1, keepdims=True), sink)
                p = jnp.exp(s - m)
                den = jnp.sum(p, axis=-1, keepdims=True) + jnp.exp(sink - m)
                o = o + _dot((p / den).astype(BF16), vw[w])
            o_ref[0, pl.ds(q0, ATT_Q), pair * LANE:(pair + 1) * LANE] = o.astype(BF16)
        return carry

    lax.fori_loop(0, S // ATT_Q, qblock, 0)


def _swa(slopes, sinks, qa, ka, valo, vahi):
    B, S, _ = qa.shape
    smem = pl.BlockSpec(memory_space=pltpu.SMEM)
    kv = pl.BlockSpec((1, S, LANE), lambda b, h: (b, 0, h))
    return pl.pallas_call(
        _swa_body,
        grid=(B, A_KV_HEADS),
        in_specs=[smem, smem, pl.BlockSpec((1, S, A_GROUP * LANE), lambda b, h: (b, 0, h)), kv, kv, kv],
        out_specs=pl.BlockSpec((1, S, A_GROUP * A_HEAD_DIM), lambda b, h: (b, 0, h)),
        out_shape=jax.ShapeDtypeStruct((B, S, A_HEADS * A_HEAD_DIM), BF16),
        compiler_params=_cparams(("parallel", "parallel")),
        name="swa_attention",
    )(slopes, sinks, qa, ka, valo, vahi)


def _mla_body(q_ref, k_ref, v_ref, o_ref):
    S = q_ref.shape[1]
    T = MLA_T
    row = lax.broadcasted_iota(jnp.int32, (T, T), 0) // CHUNK
    col = lax.broadcasted_iota(jnp.int32, (T, T), 1) // CHUNK
    diag_ok = col <= row

    def qblock(i, carry):
        q0 = pl.multiple_of(i * T, T)
        out = jnp.zeros((T, LANE), F32)
        for hh in range(2):
            sl = slice(hh * LANE, (hh + 1) * LANE)
            q = q_ref[0, pl.ds(q0, T), sl]

            def update(k0, mask, state):
                m, l, acc = state
                s = _dot_nt(q, k_ref[0, pl.ds(k0, T), sl])
                if mask:
                    s = jnp.where(diag_ok, s, NEG_INF)
                m_new = jnp.maximum(m, jnp.max(s, axis=-1, keepdims=True))
                a = jnp.exp(m - m_new)
                p = jnp.exp(s - m_new)
                l = a * l + jnp.sum(p, axis=-1, keepdims=True)
                acc = a * acc + _dot(p.astype(BF16), v_ref[0, pl.ds(k0, T), sl])
                return m_new, l, acc

            init = (jnp.full((T, 1), NEG_INF, F32), jnp.zeros((T, 1), F32), jnp.zeros((T, LANE), F32))
            state = lax.fori_loop(
                0, i, lambda j, st: update(pl.multiple_of(j * T, T), False, st), init)
            _, l, acc = update(q0, True, state)
            out = out + acc / l
        o_ref[0, pl.ds(q0, T), :] = out.astype(BF16)
        return carry

    lax.fori_loop(0, S // T, qblock, 0)


def _mla(qb, kb, vb):
    B, S, _ = qb.shape
    spec = pl.BlockSpec((1, S, 2 * LANE), lambda b, p: (b, 0, p))
    return pl.pallas_call(
        _mla_body,
        grid=(B, B_HEADS // 2),
        in_specs=[spec, spec, spec],
        out_specs=pl.BlockSpec((1, S, 2 * B_V), lambda b, p: (b, 0, p)),
        out_shape=jax.ShapeDtypeStruct((B, S, B_HEADS * B_V), BF16),
        compiler_params=_cparams(("parallel", "parallel")),
        name="mla_attention",
    )(qb, kb, vb)


def _out_proj_body(x_ref, a_ref, b_ref, wa_ref, wb_ref, o_ref):
    o_ref[...] = x_ref[...] + _dot(a_ref[...], wa_ref[...]) + _dot(b_ref[...], wb_ref[...])


def _out_proj(x2, a, b, wa, wb):
    T = x2.shape[0]
    tm = ROW_TILE
    row = lambda n: pl.BlockSpec((tm, n), lambda i: (i, 0))
    full = lambda w: pl.BlockSpec(w.shape, lambda i: (0, 0))
    return pl.pallas_call(
        _out_proj_body,
        grid=(T // tm,),
        in_specs=[row(D_MODEL), row(a.shape[1]), row(b.shape[1]), full(wa), full(wb)],
        out_specs=row(D_MODEL),
        out_shape=jax.ShapeDtypeStruct((T, D_MODEL), F32),
        compiler_params=_cparams(("parallel",)),
        name="out_proj",
    )(x2, a, b, wa, wb)


def _ffn_body(te_ref, nu_ref, x_ref, g_ref, wg_ref, wu_ref, wd_ref, o_ref, h_scr, acc_scr, *,
              residual):
    i = pl.program_id(0)
    j = pl.program_id(1)

    @pl.when(i < nu_ref[0])
    def _():
        @pl.when(j == 0)
        def _():
            xf = x_ref[...]
            h_scr[...] = _rms(xf, g_ref[...]).astype(BF16)
            acc_scr[...] = xf if residual else jnp.zeros_like(xf)

        h = h_scr[...]
        gate = _dot(h, wg_ref[0])
        up = _dot(h, wu_ref[0])
        hid = (gate * jax.nn.sigmoid(gate) * up).astype(BF16)
        acc_scr[...] += _dot(hid, wd_ref[0])

        @pl.when(j == pl.num_programs(1) - 1)
        def _():
            o_ref[...] = acc_scr[...]

    @pl.when(i >= nu_ref[0])
    def _():
        o_ref[...] = jnp.zeros_like(o_ref)


def _ffn(tile_expert, n_used, xs, g, wg, wu, wd, *, residual):
    P = xs.shape[0]
    tm, tf = ROW_TILE, FF_TILE
    nj = D_FF // tf

    def tile(i, nu):
        return jnp.minimum(i, nu[0] - 1)

    def ffcol(i, j, nu):
        return jnp.where(i < nu[0], j, nj - 1)

    grid_spec = pltpu.PrefetchScalarGridSpec(
        num_scalar_prefetch=2,
        grid=(P // tm, nj),
        in_specs=[
            pl.BlockSpec((tm, D_MODEL), lambda i, j, te, nu: (tile(i, nu), 0)),
            pl.BlockSpec((1, D_MODEL), lambda i, j, te, nu: (0, 0)),
            pl.BlockSpec((1, D_MODEL, tf), lambda i, j, te, nu: (te[tile(i, nu)], 0, ffcol(i, j, nu))),
            pl.BlockSpec((1, D_MODEL, tf), lambda i, j, te, nu: (te[tile(i, nu)], 0, ffcol(i, j, nu))),
            pl.BlockSpec((1, tf, D_MODEL), lambda i, j, te, nu: (te[tile(i, nu)], ffcol(i, j, nu), 0)),
        ],
        out_specs=pl.BlockSpec((tm, D_MODEL), lambda i, j, te, nu: (i, 0)),
        scratch_shapes=[pltpu.VMEM((tm, D_MODEL), BF16), pltpu.VMEM((tm, D_MODEL), F32)],
    )
    return pl.pallas_call(
        functools.partial(_ffn_body, residual=residual),
        grid_spec=grid_spec,
        out_shape=jax.ShapeDtypeStruct((P, D_MODEL), F32),
        compiler_params=_cparams(("arbitrary", "arbitrary")),
        name="swiglu_residual" if residual else "swiglu_experts",
    )(tile_expert, n_used, xs, g, wg, wu, wd)


def _odd_in_body(x_ref, g_ref, w_ref, o_ref):
    h = _rms(x_ref[...], g_ref[...]).astype(BF16)
    n = o_ref.shape[1]
    step = 512
    for c in range(n // step):
        sl = slice(c * step, (c + 1) * step)
        o_ref[:, sl] = _dot(h, w_ref[:, sl]).astype(BF16)


def _odd_in(x2, g, w):
    T = x2.shape[0]
    tm = ROW_TILE
    n = w.shape[1]
    return pl.pallas_call(
        _odd_in_body,
        grid=(T // tm,),
        in_specs=[pl.BlockSpec((tm, D_MODEL), lambda i: (i, 0)),
                  pl.BlockSpec(g.shape, lambda i: (0, 0)),
                  pl.BlockSpec(w.shape, lambda i: (0, 0))],
        out_specs=pl.BlockSpec((tm, n), lambda i: (i, 0)),
        out_shape=jax.ShapeDtypeStruct((T, n), BF16),
        compiler_params=_cparams(("parallel",)),
        name="odd_in_proj",
    )(x2, g, w)


def _ret_conv_body(q_ref, k_ref, v_ref, gt_ref, bd_ref, cd_ref, hd_ref, gn_ref, cw_ref,
                   yc_ref, yd_ref, r_scr, u_scr):
    S = q_ref.shape[1]
    C = CHUNK
    dk = C_KEY_DIM
    scale = dk ** -0.5
    ri = lax.broadcasted_iota(jnp.int32, (C, C), 0)
    ci = lax.broadcasted_iota(jnp.int32, (C, C), 1)
    diff = (ri - ci).astype(F32)
    pos = lax.broadcasted_iota(jnp.int32, (C, 1), 0).astype(F32)
    decays = []
    for hd in range(C_HEADS):
        log_g = math.log(1.0 - 2.0 ** (-5.0 - hd))
        intra = jnp.where(diff >= 0, jnp.exp(log_g * jnp.maximum(diff, 0.0)), 0.0) * scale
        cross = jnp.exp(log_g * (pos + 1.0))
        state = jnp.exp(log_g * (C - 1.0 - pos)) * scale
        decays.append((intra, cross, state, math.exp(log_g * C)))
    r_scr[...] = jnp.zeros_like(r_scr)

    def step(n, carry):
        r0 = pl.multiple_of(n * C, C)
        for hd in range(C_HEADS):
            intra, cross, state, chunk_decay = decays[hd]
            sl = slice(hd * dk, (hd + 1) * dk)
            q = q_ref[0, pl.ds(r0, C), sl]
            k = k_ref[0, pl.ds(r0, C), sl]
            v = v_ref[0, pl.ds(r0, C), sl]
            R = r_scr[hd]
            scores = _dot_nt(q, k) * intra
            y = _dot(scores.astype(BF16), v) + _dot(q, R.astype(BF16)) * cross
            kd_t = (k.astype(F32) * state).T.astype(BF16)
            r_scr[hd] = chunk_decay * R + _dot(kd_t, v)
            mu = jnp.mean(y, axis=-1, keepdims=True)
            yc = y - mu
            var = jnp.mean(yc * yc, axis=-1, keepdims=True)
            yn = yc * lax.rsqrt(var + EPS) * gn_ref[:, sl]
            gate = gt_ref[0, pl.ds(r0, C), sl].astype(F32)
            yc_ref[0, pl.ds(r0, C), sl] = (gate * jax.nn.sigmoid(gate) * yn).astype(BF16)
        return carry

    lax.fori_loop(0, S // C, step, 0)

    halo = 8
    u_scr[0:halo, :] = jnp.zeros((halo, D_CH), F32)
    blk = 256
    for r in range(S // blk):
        rows = slice(r * blk, (r + 1) * blk)
        u_scr[halo + r * blk:halo + (r + 1) * blk, :] = (
            cd_ref[0, rows, :].astype(F32) * hd_ref[0, rows, :].astype(F32))
    for r in range(S // blk):
        acc = jnp.zeros((blk, D_CH), F32)
        for jw in range(CONV_W):
            lo = halo - (CONV_W - 1) + jw + r * blk
            acc = acc + cw_ref[jw:jw + 1, :] * u_scr[lo:lo + blk, :]
        rows = slice(r * blk, (r + 1) * blk)
        yd_ref[0, rows, :] = (bd_ref[0, rows, :].astype(F32) * acc).astype(BF16)


def _ret_conv(z, gn, conv_w):
    B, S, _ = z.shape
    col = lambda c: pl.BlockSpec((1, S, 512), lambda b: (b, 0, c))
    full = lambda a: pl.BlockSpec(a.shape, lambda b: (0, 0))
    out = pl.BlockSpec((1, S, 512), lambda b: (b, 0, 0))
    return pl.pallas_call(
        _ret_conv_body,
        grid=(B,),
        in_specs=[col(c) for c in range(7)] + [full(gn), full(conv_w)],
        out_specs=[out, out],
        out_shape=[jax.ShapeDtypeStruct((B, S, 512), BF16)] * 2,
        scratch_shapes=[pltpu.VMEM((C_HEADS, C_KEY_DIM, C_VAL_DIM), F32),
                        pltpu.VMEM((S + 8, D_CH), F32)],
        compiler_params=_cparams(("parallel",)),
        name="retention_conv",
    )(z, z, z, z, z, z, z, gn, conv_w)


def _router_body(x_ref, g_ref, wr_ref, o_ref):
    h = _rms(x_ref[...], g_ref[...])
    logits = jnp.dot(h, wr_ref[...], preferred_element_type=F32, precision=lax.Precision.HIGHEST)
    lane = lax.broadcasted_iota(jnp.int32, logits.shape, 1)
    logits = jnp.where(lane < N_EXPERTS, logits, NEG_INF)
    m1 = jnp.max(logits, axis=-1, keepdims=True)
    i1 = jnp.min(jnp.where(logits == m1, lane, LANE), axis=-1, keepdims=True)
    rest = jnp.where(lane == i1, NEG_INF, logits)
    m2 = jnp.max(rest, axis=-1, keepdims=True)
    i2 = jnp.min(jnp.where(rest == m2, lane, LANE), axis=-1, keepdims=True)
    e2 = jnp.exp(m2 - m1)
    g1 = 1.0 / (1.0 + e2)
    g2 = e2 / (1.0 + e2)
    o_ref[...] = jnp.where(lane == 0, i1.astype(F32),
                           jnp.where(lane == 1, i2.astype(F32),
                                     jnp.where(lane == 2, g1, jnp.where(lane == 3, g2, 0.0))))


def _router(x2, g, wr):
    T = x2.shape[0]
    tm = ROW_TILE
    return pl.pallas_call(
        _router_body,
        grid=(T // tm,),
        in_specs=[pl.BlockSpec((tm, D_MODEL), lambda i: (i, 0)),
                  pl.BlockSpec(g.shape, lambda i: (0, 0)),
                  pl.BlockSpec(wr.shape, lambda i: (0, 0))],
        out_specs=pl.BlockSpec((tm, LANE), lambda i: (i, 0)),
        out_shape=jax.ShapeDtypeStruct((T, LANE), F32),
        compiler_params=_cparams(("parallel",)),
        name="router_top2",
    )(x2, g, wr)


def _row_copy(src_hbm, dst_vmem, sem, src_row, dst_row):
    return pltpu.make_async_copy(src_hbm.at[pl.ds(src_row, 1)], dst_vmem.at[pl.ds(dst_row, 1)], sem)


def _gather_body(tok_ref, nu_ref, x_hbm, o_ref, sem):
    i = pl.program_id(0)
    n = o_ref.shape[0]

    @pl.when(i < nu_ref[0])
    def _():
        base = i * n

        def issue(r, carry):
            _row_copy(x_hbm, o_ref, sem, tok_ref[base + r], r).start()
            return carry

        lax.fori_loop(0, n, issue, 0, unroll=8)
        pltpu.make_async_copy(x_hbm.at[pl.ds(0, n)], o_ref, sem).wait()

    @pl.when(i >= nu_ref[0])
    def _():
        o_ref[...] = jnp.zeros_like(o_ref)


def _gather(tok_of_slot, n_used_blocks, x2):
    P = tok_of_slot.shape[0]
    n = GATHER_ROWS
    grid_spec = pltpu.PrefetchScalarGridSpec(
        num_scalar_prefetch=2,
        grid=(P // n,),
        in_specs=[pl.BlockSpec(memory_space=pl.ANY)],
        out_specs=pl.BlockSpec((n, D_MODEL), lambda i, tok, nu: (i, 0)),
        scratch_shapes=[pltpu.SemaphoreType.DMA(())],
    )
    return pl.pallas_call(
        _gather_body,
        grid_spec=grid_spec,
        out_shape=jax.ShapeDtypeStruct((P, D_MODEL), F32),
        compiler_params=_cparams(("arbitrary",)),
        name="dispatch_gather",
    )(tok_of_slot, n_used_blocks, x2)


def _combine_body(pos_ref, x_ref, route_ref, g_ref, ys_hbm, o_ref, buf, sem):
    i = pl.program_id(0)
    n = x_ref.shape[0]
    base = i * n

    def issue(r, carry):
        for k in range(TOP_K):
            _row_copy(ys_hbm, buf.at[k], sem, pos_ref[(base + r) * TOP_K + k], r).start()
        return carry

    lax.fori_loop(0, n, issue, 0, unroll=8)
    for k in range(TOP_K):
        pltpu.make_async_copy(ys_hbm.at[pl.ds(0, n)], buf.at[k], sem).wait()
    route = route_ref[...]
    y = x_ref[...] + route[:, 2:3] * buf[0] + route[:, 3:4] * buf[1]
    o_ref[...] = _rms(y, g_ref[...])


def _combine(pos, x2, route, g, ys):
    T = x2.shape[0]
    n = GATHER_ROWS
    grid_spec = pltpu.PrefetchScalarGridSpec(
        num_scalar_prefetch=1,
        grid=(T // n,),
        in_specs=[pl.BlockSpec((n, D_MODEL), lambda i, p: (i, 0)),
                  pl.BlockSpec((n, LANE), lambda i, p: (i, 0)),
                  pl.BlockSpec((1, D_MODEL), lambda i, p: (0, 0)),
                  pl.BlockSpec(memory_space=pl.ANY)],
        out_specs=pl.BlockSpec((n, D_MODEL), lambda i, p: (i, 0)),
        scratch_shapes=[pltpu.VMEM((TOP_K, n, D_MODEL), F32), pltpu.SemaphoreType.DMA(())],
    )
    return pl.pallas_call(
        _combine_body,
        grid_spec=grid_spec,
        out_shape=jax.ShapeDtypeStruct((T, D_MODEL), F32),
        compiler_params=_cparams(("arbitrary",)),
        name="combine_norm",
    )(pos, x2, route, g, ys)


def _pad_heads(w, heads, width, offset=0):
    k = w.shape[0]
    w = w.reshape(k, heads, width)
    w = jnp.pad(w, ((0, 0), (0, 0), (offset, LANE - width - offset)))
    return w.reshape(k, heads * LANE)


def _rot_half(w):
    half = w.shape[-1] // 2
    return jnp.concatenate([-w[..., half:], w[..., :half]], axis=-1)


def _even_weights(w_in, w_uq, w_ukv):
    sizes = (A_HEADS * A_HEAD_DIM, A_KV_HEADS * A_HEAD_DIM, A_KV_HEADS * A_HEAD_DIM,
             B_Q_LORA, B_KV_LORA, B_ROPE)
    offs = np.cumsum((0,) + sizes)
    wqa, wka, wva, wcq, wckv, wkr = [w_in[:, offs[k]:offs[k + 1]] for k in range(6)]
    w_in_cat = jnp.concatenate([
        _pad_heads(wqa, A_HEADS, A_HEAD_DIM),
        _pad_heads(wka, A_KV_HEADS, A_HEAD_DIM),
        _pad_heads(wva, A_KV_HEADS, A_HEAD_DIM),
        _pad_heads(wva, A_KV_HEADS, A_HEAD_DIM, offset=A_HEAD_DIM),
        wcq, wckv,
        _pad_heads(wkr, 1, B_ROPE, offset=B_NOPE),
        _pad_heads(_rot_half(wkr), 1, B_ROPE, offset=B_NOPE),
    ], axis=1).astype(BF16)
    uq = w_uq.reshape(B_Q_LORA, B_HEADS, B_NOPE + B_ROPE)
    pad_tail = ((0, 0), (0, 0), (0, LANE - B_NOPE - B_ROPE))
    wuq_cat = jnp.pad(uq, pad_tail).reshape(B_Q_LORA, B_HEADS * LANE).astype(BF16)
    uq_rot = jnp.concatenate([jnp.zeros_like(uq[..., :B_NOPE]), _rot_half(uq[..., B_NOPE:])], axis=-1)
    wuq_rot = jnp.pad(uq_rot, pad_tail).reshape(B_Q_LORA, B_HEADS * LANE).astype(BF16)
    ukv = w_ukv.reshape(B_KV_LORA, B_HEADS, B_NOPE + B_V)
    wuk_cat = jnp.pad(ukv[..., :B_NOPE], ((0, 0), (0, 0), (0, LANE - B_NOPE)))
    wuk_cat = wuk_cat.reshape(B_KV_LORA, B_HEADS * LANE).astype(BF16)
    v = ukv[..., B_NOPE:]
    zeros = jnp.zeros_like(v)
    even = jnp.concatenate([v, zeros], axis=-1)
    odd = jnp.concatenate([zeros, v], axis=-1)
    is_even = (jnp.arange(B_HEADS) % 2 == 0)[None, :, None]
    wuv = jnp.where(is_even, even, odd).reshape(B_KV_LORA, B_HEADS * LANE).astype(BF16)
    return w_in_cat, wuq_cat, wuq_rot, wuk_cat, wuv


def _routing(route, n_rows_pad):
    T = route.shape[0]
    tm = ROW_TILE
    e_flat = route[:, :TOP_K].astype(jnp.int32).reshape(-1)
    onehot = (e_flat[:, None] == jnp.arange(N_EXPERTS, dtype=jnp.int32)[None, :]).astype(jnp.int32)
    csum = jnp.cumsum(onehot, axis=0)
    rank = jnp.sum(csum * onehot, axis=1) - 1
    counts = csum[-1]
    tiles_e = (counts + tm - 1) // tm
    tiles_end = jnp.cumsum(tiles_e)
    row_off = (tiles_end - tiles_e) * tm
    pos = (jnp.sum(onehot * row_off[None, :], axis=1) + rank).astype(jnp.int32)
    n_tiles = n_rows_pad // tm
    tile_expert = jnp.sum(jnp.arange(n_tiles, dtype=jnp.int32)[:, None] >= tiles_end[None, :], axis=1)
    tile_expert = jnp.minimum(tile_expert, N_EXPERTS - 1).astype(jnp.int32)
    n_used = tiles_end[-1:].astype(jnp.int32)
    tok = jnp.zeros((n_rows_pad,), jnp.int32).at[pos].set(
        jnp.arange(T * TOP_K, dtype=jnp.int32) // TOP_K)
    return pos, tile_expert, n_used, tok


def kernel(x, positions, even_norm_mix, even_w_in, even_sinks, even_q_norm, even_w_uq, even_kv_norm, even_w_ukv, even_w_out, even_norm_ffn, even_w_gate, even_w_up, even_w_down, odd_norm_mix, odd_w_in, odd_ret_gn, odd_conv_w, odd_w_out, odd_norm_ffn, odd_router, odd_we_gate, odd_we_up, odd_we_down, final_norm):
    B, S, D = x.shape
    T = B * S
    tm = ROW_TILE
    x2 = x.reshape(T, D)
    pos2 = positions.reshape(T, 1)

    inv_freq = ROPE_THETA ** (-np.arange(0, B_ROPE, 2, dtype=np.float32) / B_ROPE)
    inv_row = np.zeros((1, LANE), np.float32)
    inv_row[0, B_NOPE:B_NOPE + B_ROPE // 2] = inv_freq
    inv_row[0, B_NOPE + B_ROPE // 2:B_NOPE + B_ROPE] = inv_freq
    slopes = jnp.asarray(2.0 ** (-8.0 * (np.arange(A_HEADS, dtype=np.float32) + 1.0) / A_HEADS), F32)
    dense_tiles = jnp.zeros((T // tm,), jnp.int32)
    dense_used = jnp.full((1,), T // tm, jnp.int32)

    w_in_cat, wuq_cat, wuq_rot, wuk_cat, wuv = _even_weights(even_w_in[0], even_w_uq[0], even_w_ukv[0])
    qa, ka, valo, vahi, qb, kb, vb = _even_in(
        x2, pos2, even_norm_mix[0][None], w_in_cat, jnp.asarray(inv_row),
        even_q_norm[0][None], wuq_cat, wuq_rot, even_kv_norm[0][None], wuk_cat, wuv)
    r3 = lambda t: t.reshape(B, S, t.shape[-1])
    ya = _swa(slopes, even_sinks[0], r3(qa), r3(ka), r3(valo), r3(vahi)).reshape(T, -1)
    yb = _mla(r3(qb), r3(kb), r3(vb)).reshape(T, -1)
    w_out = even_w_out[0].astype(BF16)
    x2 = _out_proj(x2, ya, yb, w_out[:ya.shape[1]], w_out[ya.shape[1]:])
    x2 = _ffn(dense_tiles, dense_used, x2, even_norm_ffn[0][None],
              even_w_gate.astype(BF16), even_w_up.astype(BF16), even_w_down.astype(BF16),
              residual=True)

    z = _odd_in(x2, odd_norm_mix[0][None], odd_w_in[0].astype(BF16))
    yc, yd = _ret_conv(r3(z), odd_ret_gn[0][None], odd_conv_w[0])
    w_out = odd_w_out[0].astype(BF16)
    x2 = _out_proj(x2, yc.reshape(T, -1), yd.reshape(T, -1), w_out[:C_HEADS * C_VAL_DIM],
                   w_out[C_HEADS * C_VAL_DIM:])
    wr = jnp.pad(odd_router[0], ((0, 0), (0, LANE - N_EXPERTS)))
    route = _router(x2, odd_norm_ffn[0][None], wr)
    n_rows_pad = T * TOP_K + N_EXPERTS * tm
    pos, tile_expert, n_used, tok = _routing(route, n_rows_pad)
    xs = _gather(tok, n_used * (tm // GATHER_ROWS), x2)
    ys = _ffn(tile_expert, n_used, xs, odd_norm_ffn[0][None],
              odd_we_gate[0].astype(BF16), odd_we_up[0].astype(BF16), odd_we_down[0].astype(BF16),
              residual=False)
    out = _combine(pos, x2, route, final_norm[None], ys)
    return out.reshape(B, S, D)
```

```python
import functools
import math

import numpy as np
import jax
import jax.numpy as jnp
from jax import lax
from jax.experimental import pallas as pl
from jax.experimental.pallas import tpu as pltpu

F32 = jnp.float32
BF16 = jnp.bfloat16

D_MODEL = 1024
CHUNK = 64
A_HEADS = 8
A_KV_HEADS = 2
A_HEAD_DIM = 64
A_GROUP = A_HEADS // A_KV_HEADS
WINDOW_CHUNKS = 2
B_HEADS = 8
B_Q_LORA = 384
B_KV_LORA = 256
B_NOPE = 64
B_ROPE = 32
B_V = 64
ROPE_THETA = 10000.0
C_HEADS = 4
C_KEY_DIM = 128
C_VAL_DIM = 128
D_CH = 512
CONV_W = 3
D_FF = 3584
N_EXPERTS = 8
TOP_K = 2
EPS = 1e-6
NEG_INF = -1e30
LOG2E = math.log2(math.e)

LANE = 128
ROW_TILE = 512
FF_TILE = 1792
ATT_Q = 128
MLA_T = 256
GATHER_ROWS = 256
VMEM_LIMIT = 56 * 1024 * 1024

_EV_QA = A_HEADS * LANE
_EV_KA = A_KV_HEADS * LANE
_EV_VA = A_KV_HEADS * LANE
_EV_OFF = np.cumsum([0, _EV_QA, _EV_KA, _EV_VA, _EV_VA, B_Q_LORA, B_KV_LORA, LANE, LANE])
_EV_COLS = int(_EV_OFF[-1])


def _cparams(semantics):
    return pltpu.CompilerParams(dimension_semantics=semantics, vmem_limit_bytes=VMEM_LIMIT)


def _rms(xf, g):
    return xf * lax.rsqrt(jnp.mean(xf * xf, axis=-1, keepdims=True) + EPS) * g


def _dot(a, b):
    return jnp.dot(a, b, preferred_element_type=F32)


def _dot_nt(a, b):
    return lax.dot_general(a, b, (((1,), (1,)), ((), ())), preferred_element_type=F32)


def _even_in_body(x_ref, pos_ref, g_ref, win_ref, invf_ref, qn_ref, wuqc_ref, wuqr_ref,
                  kvn_ref, wukc_ref, wuv_ref,
                  qa_ref, ka_ref, valo_ref, vahi_ref, qb_ref, kb_ref, vb_ref):
    h = _rms(x_ref[...], g_ref[...]).astype(BF16)

    def proj(k):
        return _dot(h, win_ref[:, int(_EV_OFF[k]):int(_EV_OFF[k + 1])])

    qa_ref[...] = (proj(0) * (A_HEAD_DIM ** -0.5 * LOG2E)).astype(BF16)
    ka_ref[...] = proj(1).astype(BF16)
    valo_ref[...] = proj(2).astype(BF16)
    vahi_ref[...] = proj(3).astype(BF16)

    ang = pos_ref[...].astype(F32) * invf_ref[...]
    lane = lax.broadcasted_iota(jnp.int32, ang.shape, 1)
    is_rope = (lane >= B_NOPE) & (lane < B_NOPE + B_ROPE)
    cosm = jnp.where(lane < B_NOPE, 1.0, jnp.where(is_rope, jnp.cos(ang), 0.0))
    sinm = jnp.where(is_rope, jnp.sin(ang), 0.0)

    cq = _rms(proj(4), qn_ref[...]).astype(BF16)
    q_main = _dot(cq, wuqc_ref[...])
    q_rot = _dot(cq, wuqr_ref[...])
    k_rope = proj(6) * cosm + proj(7) * sinm
    ckv = _rms(proj(5), kvn_ref[...]).astype(BF16)
    k_main = _dot(ckv, wukc_ref[...])
    scale = (B_NOPE + B_ROPE) ** -0.5 * LOG2E
    for hd in range(B_HEADS):
        sl = slice(hd * LANE, (hd + 1) * LANE)
        qb_ref[:, sl] = ((q_main[:, sl] * cosm + q_rot[:, sl] * sinm) * scale).astype(BF16)
        kb_ref[:, sl] = (k_main[:, sl] + k_rope).astype(BF16)
    vb_ref[...] = _dot(ckv, wuv_ref[...]).astype(BF16)


def _even_in(x2, pos2, g, w_in, inv_freq_row, q_norm, wuq_cat, wuq_rot, kv_norm, wuk_cat, wuv):
    T = x2.shape[0]
    tm = ROW_TILE
    row = lambda n: pl.BlockSpec((tm, n), lambda i: (i, 0))
    full = lambda a: pl.BlockSpec(a.shape, lambda i: (0, 0))
    outs = [_EV_QA, _EV_KA, _EV_VA, _EV_VA, B_HEADS * LANE, B_HEADS * LANE, B_HEADS * LANE]
    return pl.pallas_call(
        _even_in_body,
        grid=(T // tm,),
        in_specs=[row(D_MODEL), row(1), full(g), full(w_in), full(inv_freq_row), full(q_norm),
                  full(wuq_cat), full(wuq_rot), full(kv_norm), full(wuk_cat), full(wuv)],
        out_specs=[row(n) for n in outs],
        out_shape=[jax.ShapeDtypeStruct((T, n), BF16) for n in outs],
        compiler_params=_cparams(("parallel",)),
        name="even_in_proj",
    )(x2, pos2, g, w_in, inv_freq_row, q_norm, wuq_cat, wuq_rot, kv_norm, wuk_cat, wuv)


def _swa_body(slopes_ref, sinks_ref, q_ref, k_ref, vlo_ref, vhi_ref, o_ref):
    hk = pl.program_id(1)
    S = q_ref.shape[1]
    win = ATT_Q + WINDOW_CHUNKS * CHUNK
    slope = [slopes_ref[hk * A_GROUP + g] for g in range(A_GROUP)]
    sink = [sinks_ref[hk * A_GROUP + g] for g in range(A_GROUP)]

    def mask_bias(delta):
        qpos = delta + lax.broadcasted_iota(jnp.int32, (ATT_Q, win), 0)
        kpos = lax.broadcasted_iota(jnp.int32, (ATT_Q, win), 1)
        qc = qpos // CHUNK
        kc = kpos // CHUNK
        valid = (kc <= qc) & (kc >= qc - WINDOW_CHUNKS)
        dist = jnp.abs(qpos - kpos).astype(F32)
        return [jnp.where(valid, -slope[g] * dist, NEG_INF) for g in range(A_GROUP)]

    bias_by_delta = {0: mask_bias(0), WINDOW_CHUNKS * CHUNK: mask_bias(WINDOW_CHUNKS * CHUNK)}
    for i in range(S // ATT_Q):
        q0 = i * ATT_Q
        k0 = max(q0 - WINDOW_CHUNKS * CHUNK, 0)
        bias = bias_by_delta[q0 - k0]
        kw = k_ref[0, k0:k0 + win, :]
        vw = (vlo_ref[0, k0:k0 + win, :], vhi_ref[0, k0:k0 + win, :])
        for pair in range(A_GROUP // 2):
            o = jnp.zeros((ATT_Q, LANE), F32)
            for w in range(2):
                g = pair * 2 + w
                s = _dot_nt(q_ref[0, q0:q0 + ATT_Q, g * LANE:(g + 1) * LANE], kw) + bias[g]
                m = jnp.maximum(jnp.max(s, axis=-1, keepdims=True), sink[g])
                p = jnp.exp2(s - m)
                den = jnp.sum(p, axis=-1, keepdims=True) + jnp.exp2(sink[g] - m)
                o = o + _dot(p.astype(BF16), vw[w]) * (1.0 / den)
            o_ref[0, q0:q0 + ATT_Q, pair * LANE:(pair + 1) * LANE] = o.astype(BF16)


def _swa(slopes, sinks, qa, ka, valo, vahi):
    B, S, _ = qa.shape
    smem = pl.BlockSpec(memory_space=pltpu.SMEM)
    kv = pl.BlockSpec((1, S, LANE), lambda b, h: (b, 0, h))
    return pl.pallas_call(
        _swa_body,
        grid=(B, A_KV_HEADS),
        in_specs=[smem, smem, pl.BlockSpec((1, S, A_GROUP * LANE), lambda b, h: (b, 0, h)), kv, kv, kv],
        out_specs=pl.BlockSpec((1, S, A_GROUP * A_HEAD_DIM), lambda b, h: (b, 0, h)),
        out_shape=jax.ShapeDtypeStruct((B, S, A_HEADS * A_HEAD_DIM), BF16),
        compiler_params=_cparams(("parallel", "parallel")),
        name="swa_attention",
    )(slopes, sinks, qa, ka, valo, vahi)


def _mla_body(q_ref, k_ref, v_ref, o_ref):
    S = q_ref.shape[1]
    T = MLA_T
    row = lax.broadcasted_iota(jnp.int32, (T, T), 0) // CHUNK
    col = lax.broadcasted_iota(jnp.int32, (T, T), 1) // CHUNK
    diag_ok = col <= row

    for i in range(S // T):
        q0 = i * T
        out = jnp.zeros((T, LANE), F32)
        for hh in range(2):
            sl = slice(hh * LANE, (hh + 1) * LANE)
            q = q_ref[0, q0:q0 + T, sl]
            s_diag = jnp.where(diag_ok, _dot_nt(q, k_ref[0, q0:q0 + T, sl]), NEG_INF)
            m = jnp.max(s_diag, axis=-1, keepdims=True)
            if i > 0:
                s_past = _dot_nt(q, k_ref[0, 0:q0, sl])
                m = jnp.maximum(m, jnp.max(s_past, axis=-1, keepdims=True))
            p = jnp.exp2(s_diag - m)
            den = jnp.sum(p, axis=-1, keepdims=True)
            acc = _dot(p.astype(BF16), v_ref[0, q0:q0 + T, sl])
            if i > 0:
                p = jnp.exp2(s_past - m)
                den = den + jnp.sum(p, axis=-1, keepdims=True)
                acc = acc + _dot(p.astype(BF16), v_ref[0, 0:q0, sl])
            out = out + acc * (1.0 / den)
        o_ref[0, q0:q0 + T, :] = out.astype(BF16)


def _mla(qb, kb, vb):
    B, S, _ = qb.shape
    spec = pl.BlockSpec((1, S, 2 * LANE), lambda b, p: (b, 0, p))
    return pl.pallas_call(
        _mla_body,
        grid=(B, B_HEADS // 2),
        in_specs=[spec, spec, spec],
        out_specs=pl.BlockSpec((1, S, 2 * B_V), lambda b, p: (b, 0, p)),
        out_shape=jax.ShapeDtypeStruct((B, S, B_HEADS * B_V), BF16),
        compiler_params=_cparams(("parallel", "parallel")),
        name="mla_attention",
    )(qb, kb, vb)


def _out_proj_body(x_ref, a_ref, b_ref, wa_ref, wb_ref, o_ref):
    o_ref[...] = x_ref[...] + _dot(a_ref[...], wa_ref[...]) + _dot(b_ref[...], wb_ref[...])


def _out_proj(x2, a, b, wa, wb):
    T = x2.shape[0]
    tm = ROW_TILE
    row = lambda n: pl.BlockSpec((tm, n), lambda i: (i, 0))
    full = lambda w: pl.BlockSpec(w.shape, lambda i: (0, 0))
    return pl.pallas_call(
        _out_proj_body,
        grid=(T // tm,),
        in_specs=[row(D_MODEL), row(a.shape[1]), row(b.shape[1]), full(wa), full(wb)],
        out_specs=row(D_MODEL),
        out_shape=jax.ShapeDtypeStruct((T, D_MODEL), F32),
        compiler_params=_cparams(("parallel",)),
        name="out_proj",
    )(x2, a, b, wa, wb)


def _ffn_body(te_ref, nu_ref, x_ref, g_ref, wg_ref, wu_ref, wd_ref, o_ref, h_scr, acc_scr, *,
              residual):
    i = pl.program_id(0)
    j = pl.program_id(1)

    @pl.when(i < nu_ref[0])
    def _():
        @pl.when(j == 0)
        def _():
            xf = x_ref[...]
            h_scr[...] = _rms(xf, g_ref[...]).astype(BF16)
            acc_scr[...] = xf if residual else jnp.zeros_like(xf)

        h = h_scr[...]
        gate = _dot(h, wg_ref[0])
        up = _dot(h, wu_ref[0])
        hid = (gate * jax.nn.sigmoid(gate) * up).astype(BF16)
        acc_scr[...] += _dot(hid, wd_ref[0])

        @pl.when(j == pl.num_programs(1) - 1)
        def _():
            o_ref[...] = acc_scr[...]

    @pl.when(i >= nu_ref[0])
    def _():
        o_ref[...] = jnp.zeros_like(o_ref)


def _ffn(tile_expert, n_used, xs, g, wg, wu, wd, *, residual):
    P = xs.shape[0]
    tm, tf = ROW_TILE, FF_TILE
    nj = D_FF // tf

    def tile(i, nu):
        return jnp.minimum(i, nu[0] - 1)

    def ffcol(i, j, nu):
        return jnp.where(i < nu[0], j, nj - 1)

    grid_spec = pltpu.PrefetchScalarGridSpec(
        num_scalar_prefetch=2,
        grid=(P // tm, nj),
        in_specs=[
            pl.BlockSpec((tm, D_MODEL), lambda i, j, te, nu: (tile(i, nu), 0)),
            pl.BlockSpec((1, D_MODEL), lambda i, j, te, nu: (0, 0)),
            pl.BlockSpec((1, D_MODEL, tf), lambda i, j, te, nu: (te[tile(i, nu)], 0, ffcol(i, j, nu))),
            pl.BlockSpec((1, D_MODEL, tf), lambda i, j, te, nu: (te[tile(i, nu)], 0, ffcol(i, j, nu))),
            pl.BlockSpec((1, tf, D_MODEL), lambda i, j, te, nu: (te[tile(i, nu)], ffcol(i, j, nu), 0)),
        ],
        out_specs=pl.BlockSpec((tm, D_MODEL), lambda i, j, te, nu: (i, 0)),
        scratch_shapes=[pltpu.VMEM((tm, D_MODEL), BF16), pltpu.VMEM((tm, D_MODEL), F32)],
    )
    return pl.pallas_call(
        functools.partial(_ffn_body, residual=residual),
        grid_spec=grid_spec,
        out_shape=jax.ShapeDtypeStruct((P, D_MODEL), F32),
        compiler_params=_cparams(("arbitrary", "arbitrary")),
        name="swiglu_residual" if residual else "swiglu_experts",
    )(tile_expert, n_used, xs, g, wg, wu, wd)


def _odd_in_body(x_ref, g_ref, w_ref, o_ref):
    h = _rms(x_ref[...], g_ref[...]).astype(BF16)
    n = o_ref.shape[1]
    step = 512
    for c in range(n // step):
        sl = slice(c * step, (c + 1) * step)
        o_ref[:, sl] = _dot(h, w_ref[:, sl]).astype(BF16)


def _odd_in(x2, g, w):
    T = x2.shape[0]
    tm = ROW_TILE
    n = w.shape[1]
    return pl.pallas_call(
        _odd_in_body,
        grid=(T // tm,),
        in_specs=[pl.BlockSpec((tm, D_MODEL), lambda i: (i, 0)),
                  pl.BlockSpec(g.shape, lambda i: (0, 0)),
                  pl.BlockSpec(w.shape, lambda i: (0, 0))],
        out_specs=pl.BlockSpec((tm, n), lambda i: (i, 0)),
        out_shape=jax.ShapeDtypeStruct((T, n), BF16),
        compiler_params=_cparams(("parallel",)),
        name="odd_in_proj",
    )(x2, g, w)


def _ret_conv_body(q_ref, k_ref, v_ref, gt_ref, bd_ref, cd_ref, hd_ref, gn_ref, cw_ref,
                   yc_ref, yd_ref, r_scr, u_scr):
    S = q_ref.shape[1]
    C = CHUNK
    dk = C_KEY_DIM
    scale = dk ** -0.5
    ri = lax.broadcasted_iota(jnp.int32, (C, C), 0)
    ci = lax.broadcasted_iota(jnp.int32, (C, C), 1)
    diff = (ri - ci).astype(F32)
    pos = lax.broadcasted_iota(jnp.int32, (C, 1), 0).astype(F32)
    decays = []
    for hd in range(C_HEADS):
        log_g = math.log(1.0 - 2.0 ** (-5.0 - hd))
        intra = jnp.where(diff >= 0, jnp.exp(log_g * jnp.maximum(diff, 0.0)), 0.0) * scale
        cross = jnp.exp(log_g * (pos + 1.0))
        state = jnp.exp(log_g * (C - 1.0 - pos)) * scale
        decays.append((intra, cross, state, math.exp(log_g * C)))
    r_scr[...] = jnp.zeros_like(r_scr)

    def step(n, carry):
        r0 = pl.multiple_of(n * C, C)
        for hd in range(C_HEADS):
            intra, cross, state, chunk_decay = decays[hd]
            sl = slice(hd * dk, (hd + 1) * dk)
            q = q_ref[0, pl.ds(r0, C), sl]
            k = k_ref[0, pl.ds(r0, C), sl]
            v = v_ref[0, pl.ds(r0, C), sl]
            R = r_scr[hd]
            scores = _dot_nt(q, k) * intra
            y = _dot(scores.astype(BF16), v) + _dot(q, R.astype(BF16)) * cross
            kd_t = (k.astype(F32) * state).T.astype(BF16)
            r_scr[hd] = chunk_decay * R + _dot(kd_t, v)
            mu = jnp.mean(y, axis=-1, keepdims=True)
            yc = y - mu
            var = jnp.mean(yc * yc, axis=-1, keepdims=True)
            yn = yc * lax.rsqrt(var + EPS) * gn_ref[:, sl]
            gate = gt_ref[0, pl.ds(r0, C), sl].astype(F32)
            yc_ref[0, pl.ds(r0, C), sl] = (gate * jax.nn.sigmoid(gate) * yn).astype(BF16)
        return carry

    lax.fori_loop(0, S // C, step, 0)

    halo = 8
    u_scr[0:halo, :] = jnp.zeros((halo, D_CH), F32)
    blk = 256
    for r in range(S // blk):
        rows = slice(r * blk, (r + 1) * blk)
        u_scr[halo + r * blk:halo + (r + 1) * blk, :] = (
            cd_ref[0, rows, :].astype(F32) * hd_ref[0, rows, :].astype(F32))
    for r in range(S // blk):
        acc = jnp.zeros((blk, D_CH), F32)
        for jw in range(CONV_W):
            lo = halo - (CONV_W - 1) + jw + r * blk
            acc = acc + cw_ref[jw:jw + 1, :] * u_scr[lo:lo + blk, :]
        rows = slice(r * blk, (r + 1) * blk)
        yd_ref[0, rows, :] = (bd_ref[0, rows, :].astype(F32) * acc).astype(BF16)


def _ret_conv(z, gn, conv_w):
    B, S, _ = z.shape
    col = lambda c: pl.BlockSpec((1, S, 512), lambda b: (b, 0, c))
    full = lambda a: pl.BlockSpec(a.shape, lambda b: (0, 0))
    out = pl.BlockSpec((1, S, 512), lambda b: (b, 0, 0))
    return pl.pallas_call(
        _ret_conv_body,
        grid=(B,),
        in_specs=[col(c) for c in range(7)] + [full(gn), full(conv_w)],
        out_specs=[out, out],
        out_shape=[jax.ShapeDtypeStruct((B, S, 512), BF16)] * 2,
        scratch_shapes=[pltpu.VMEM((C_HEADS, C_KEY_DIM, C_VAL_DIM), F32),
                        pltpu.VMEM((S + 8, D_CH), F32)],
        compiler_params=_cparams(("parallel",)),
        name="retention_conv",
    )(z, z, z, z, z, z, z, gn, conv_w)


def _router_body(x_ref, g_ref, wr_ref, o_ref):
    h = _rms(x_ref[...], g_ref[...])
    logits = jnp.dot(h, wr_ref[...], preferred_element_type=F32, precision=lax.Precision.HIGHEST)
    lane = lax.broadcasted_iota(jnp.int32, logits.shape, 1)
    logits = jnp.where(lane < N_EXPERTS, logits, NEG_INF)
    m1 = jnp.max(logits, axis=-1, keepdims=True)
    i1 = jnp.min(jnp.where(logits == m1, lane, LANE), axis=-1, keepdims=True)
    rest = jnp.where(lane == i1, NEG_INF, logits)
    m2 = jnp.max(rest, axis=-1, keepdims=True)
    i2 = jnp.min(jnp.where(rest == m2, lane, LANE), axis=-1, keepdims=True)
    e2 = jnp.exp(m2 - m1)
    g1 = 1.0 / (1.0 + e2)
    g2 = e2 / (1.0 + e2)
    o_ref[...] = jnp.where(lane == 0, i1.astype(F32),
                           jnp.where(lane == 1, i2.astype(F32),
                                     jnp.where(lane == 2, g1, jnp.where(lane == 3, g2, 0.0))))


def _router(x2, g, wr):
    T = x2.shape[0]
    tm = ROW_TILE
    return pl.pallas_call(
        _router_body,
        grid=(T // tm,),
        in_specs=[pl.BlockSpec((tm, D_MODEL), lambda i: (i, 0)),
                  pl.BlockSpec(g.shape, lambda i: (0, 0)),
                  pl.BlockSpec(wr.shape, lambda i: (0, 0))],
        out_specs=pl.BlockSpec((tm, LANE), lambda i: (i, 0)),
        out_shape=jax.ShapeDtypeStruct((T, LANE), F32),
        compiler_params=_cparams(("parallel",)),
        name="router_top2",
    )(x2, g, wr)


def _row_copy(src_hbm, dst_vmem, sem, src_row, dst_row):
    return pltpu.make_async_copy(src_hbm.at[pl.ds(src_row, 1)], dst_vmem.at[pl.ds(dst_row, 1)], sem)


def _dispatch_body(pos_ref, fill_ref, nu_ref, x_hbm, o_hbm, zero_scr, sem, zero_sem):
    i = pl.program_id(0)
    n = GATHER_ROWS
    tm = ROW_TILE

    def zero_tile(row0):
        fill = pltpu.make_async_copy(zero_scr, o_hbm.at[pl.ds(pl.multiple_of(row0, tm), tm)], zero_sem)
        fill.start()
        fill.wait()

    @pl.when(i == 0)
    def _():
        zero_scr[...] = jnp.zeros_like(zero_scr)
        for e in range(N_EXPERTS):
            @pl.when(fill_ref[e] >= 0)
            def _():
                zero_tile(fill_ref[e])

        def tail(t, carry):
            zero_tile(t * tm)
            return carry

        lax.fori_loop(nu_ref[0], o_hbm.shape[0] // tm, tail, 0)

    def wait_block():
        pltpu.make_async_copy(x_hbm.at[pl.ds(0, n)], o_hbm.at[pl.ds(0, n)], sem).wait()

    base = i * n

    def issue(r, carry):
        a = base + r
        _row_copy(x_hbm, o_hbm, sem, a // TOP_K, pos_ref[a]).start()
        return carry

    lax.fori_loop(0, n, issue, 0, unroll=8)

    @pl.when(i > 0)
    def _():
        wait_block()

    @pl.when(i == pl.num_programs(0) - 1)
    def _():
        wait_block()


def _dispatch(pos, fill_rows, n_used, x2, n_rows_pad):
    n = GATHER_ROWS
    grid_spec = pltpu.PrefetchScalarGridSpec(
        num_scalar_prefetch=3,
        grid=(pos.shape[0] // n,),
        in_specs=[pl.BlockSpec(memory_space=pl.ANY)],
        out_specs=pl.BlockSpec(memory_space=pl.ANY),
        scratch_shapes=[pltpu.VMEM((ROW_TILE, D_MODEL), F32), pltpu.SemaphoreType.DMA(()),
                        pltpu.SemaphoreType.DMA(())],
    )
    return pl.pallas_call(
        _dispatch_body,
        grid_spec=grid_spec,
        out_shape=jax.ShapeDtypeStruct((n_rows_pad, D_MODEL), F32),
        compiler_params=_cparams(("arbitrary",)),
        name="dispatch_rows",
    )(pos, fill_rows, n_used, x2)


def _combine_body(pos_ref, x_ref, route_ref, g_ref, ys_hbm, o_ref, buf, sem):
    i = pl.program_id(0)
    n = x_ref.shape[0]

    def fetch(block, slot):
        base = block * n

        def issue(r, carry):
            for k in range(TOP_K):
                _row_copy(ys_hbm, buf.at[slot, k], sem.at[slot],
                          pos_ref[(base + r) * TOP_K + k], r).start()
            return carry

        lax.fori_loop(0, n, issue, 0, unroll=8)

    slot = i % 2

    @pl.when(i == 0)
    def _():
        fetch(0, 0)

    @pl.when(i + 1 < pl.num_programs(0))
    def _():
        fetch(i + 1, 1 - slot)

    for k in range(TOP_K):
        pltpu.make_async_copy(ys_hbm.at[pl.ds(0, n)], buf.at[slot, k], sem.at[slot]).wait()
    route = route_ref[...]
    y = x_ref[...] + route[:, 2:3] * buf[slot, 0] + route[:, 3:4] * buf[slot, 1]
    o_ref[...] = _rms(y, g_ref[...])


def _combine(pos, x2, route, g, ys):
    T = x2.shape[0]
    n = GATHER_ROWS
    grid_spec = pltpu.PrefetchScalarGridSpec(
        num_scalar_prefetch=1,
        grid=(T // n,),
        in_specs=[pl.BlockSpec((n, D_MODEL), lambda i, p: (i, 0)),
                  pl.BlockSpec((n, LANE), lambda i, p: (i, 0)),
                  pl.BlockSpec((1, D_MODEL), lambda i, p: (0, 0)),
                  pl.BlockSpec(memory_space=pl.ANY)],
        out_specs=pl.BlockSpec((n, D_MODEL), lambda i, p: (i, 0)),
        scratch_shapes=[pltpu.VMEM((2, TOP_K, n, D_MODEL), F32), pltpu.SemaphoreType.DMA((2,))],
    )
    return pl.pallas_call(
        _combine_body,
        grid_spec=grid_spec,
        out_shape=jax.ShapeDtypeStruct((T, D_MODEL), F32),
        compiler_params=_cparams(("arbitrary",)),
        name="combine_norm",
    )(pos, x2, route, g, ys)


def _pad_heads(w, heads, width, offset=0):
    k = w.shape[0]
    w = w.reshape(k, heads, width)
    w = jnp.pad(w, ((0, 0), (0, 0), (offset, LANE - width - offset)))
    return w.reshape(k, heads * LANE)


def _rot_half(w):
    half = w.shape[-1] // 2
    return jnp.concatenate([-w[..., half:], w[..., :half]], axis=-1)


def _even_weights(w_in, w_uq, w_ukv):
    sizes = (A_HEADS * A_HEAD_DIM, A_KV_HEADS * A_HEAD_DIM, A_KV_HEADS * A_HEAD_DIM,
             B_Q_LORA, B_KV_LORA, B_ROPE)
    offs = np.cumsum((0,) + sizes)
    wqa, wka, wva, wcq, wckv, wkr = [w_in[:, offs[k]:offs[k + 1]] for k in range(6)]
    w_in_cat = jnp.concatenate([
        _pad_heads(wqa, A_HEADS, A_HEAD_DIM),
        _pad_heads(wka, A_KV_HEADS, A_HEAD_DIM),
        _pad_heads(wva, A_KV_HEADS, A_HEAD_DIM),
        _pad_heads(wva, A_KV_HEADS, A_HEAD_DIM, offset=A_HEAD_DIM),
        wcq, wckv,
        _pad_heads(wkr, 1, B_ROPE, offset=B_NOPE),
        _pad_heads(_rot_half(wkr), 1, B_ROPE, offset=B_NOPE),
    ], axis=1).astype(BF16)
    uq = w_uq.reshape(B_Q_LORA, B_HEADS, B_NOPE + B_ROPE)
    pad_tail = ((0, 0), (0, 0), (0, LANE - B_NOPE - B_ROPE))
    wuq_cat = jnp.pad(uq, pad_tail).reshape(B_Q_LORA, B_HEADS * LANE).astype(BF16)
    uq_rot = jnp.concatenate([jnp.zeros_like(uq[..., :B_NOPE]), _rot_half(uq[..., B_NOPE:])], axis=-1)
    wuq_rot = jnp.pad(uq_rot, pad_tail).reshape(B_Q_LORA, B_HEADS * LANE).astype(BF16)
    ukv = w_ukv.reshape(B_KV_LORA, B_HEADS, B_NOPE + B_V)
    wuk_cat = jnp.pad(ukv[..., :B_NOPE], ((0, 0), (0, 0), (0, LANE - B_NOPE)))
    wuk_cat = wuk_cat.reshape(B_KV_LORA, B_HEADS * LANE).astype(BF16)
    v = ukv[..., B_NOPE:]
    zeros = jnp.zeros_like(v)
    even = jnp.concatenate([v, zeros], axis=-1)
    odd = jnp.concatenate([zeros, v], axis=-1)
    is_even = (jnp.arange(B_HEADS) % 2 == 0)[None, :, None]
    wuv = jnp.where(is_even, even, odd).reshape(B_KV_LORA, B_HEADS * LANE).astype(BF16)
    return w_in_cat, wuq_cat, wuq_rot, wuk_cat, wuv


def _routing(route, n_rows_pad):
    tm = ROW_TILE
    e_flat = route[:, :TOP_K].astype(jnp.int32).reshape(-1)
    onehot = (e_flat[:, None] == jnp.arange(N_EXPERTS, dtype=jnp.int32)[None, :]).astype(jnp.int32)
    csum = jnp.cumsum(onehot, axis=0)
    rank = jnp.sum(csum * onehot, axis=1) - 1
    counts = csum[-1]
    tiles_e = (counts + tm - 1) // tm
    tiles_end = jnp.cumsum(tiles_e)
    row_off = (tiles_end - tiles_e) * tm
    pos = (jnp.sum(onehot * row_off[None, :], axis=1) + rank).astype(jnp.int32)
    n_tiles = n_rows_pad // tm
    tile_expert = jnp.sum(jnp.arange(n_tiles, dtype=jnp.int32)[:, None] >= tiles_end[None, :], axis=1)
    tile_expert = jnp.minimum(tile_expert, N_EXPERTS - 1).astype(jnp.int32)
    n_used = tiles_end[-1:].astype(jnp.int32)
    fill_rows = jnp.where(tiles_e > 0, (tiles_end - 1) * tm, -1).astype(jnp.int32)
    return pos, tile_expert, n_used, fill_rows


def kernel(x, positions, even_norm_mix, even_w_in, even_sinks, even_q_norm, even_w_uq, even_kv_norm, even_w_ukv, even_w_out, even_norm_ffn, even_w_gate, even_w_up, even_w_down, odd_norm_mix, odd_w_in, odd_ret_gn, odd_conv_w, odd_w_out, odd_norm_ffn, odd_router, odd_we_gate, odd_we_up, odd_we_down, final_norm):
    B, S, D = x.shape
    T = B * S
    tm = ROW_TILE
    x2 = x.reshape(T, D)
    pos2 = positions.reshape(T, 1)

    inv_freq = ROPE_THETA ** (-np.arange(0, B_ROPE, 2, dtype=np.float32) / B_ROPE)
    inv_row = np.zeros((1, LANE), np.float32)
    inv_row[0, B_NOPE:B_NOPE + B_ROPE // 2] = inv_freq
    inv_row[0, B_NOPE + B_ROPE // 2:B_NOPE + B_ROPE] = inv_freq
    slopes = 2.0 ** (-8.0 * (np.arange(A_HEADS, dtype=np.float32) + 1.0) / A_HEADS)
    slopes = jnp.asarray(slopes * LOG2E, F32)
    dense_tiles = jnp.zeros((T // tm,), jnp.int32)
    dense_used = jnp.full((1,), T // tm, jnp.int32)

    w_in_cat, wuq_cat, wuq_rot, wuk_cat, wuv = _even_weights(even_w_in[0], even_w_uq[0], even_w_ukv[0])
    qa, ka, valo, vahi, qb, kb, vb = _even_in(
        x2, pos2, even_norm_mix[0][None], w_in_cat, jnp.asarray(inv_row),
        even_q_norm[0][None], wuq_cat, wuq_rot, even_kv_norm[0][None], wuk_cat, wuv)
    r3 = lambda t: t.reshape(B, S, t.shape[-1])
    ya = _swa(slopes, even_sinks[0] * LOG2E, r3(qa), r3(ka), r3(valo), r3(vahi)).reshape(T, -1)
    yb = _mla(r3(qb), r3(kb), r3(vb)).reshape(T, -1)
    w_out = even_w_out[0].astype(BF16)
    x2 = _out_proj(x2, ya, yb, w_out[:ya.shape[1]], w_out[ya.shape[1]:])
    x2 = _ffn(dense_tiles, dense_used, x2, even_norm_ffn[0][None],
              even_w_gate.astype(BF16), even_w_up.astype(BF16), even_w_down.astype(BF16),
              residual=True)

    z = _odd_in(x2, odd_norm_mix[0][None], odd_w_in[0].astype(BF16))
    yc, yd = _ret_conv(r3(z), odd_ret_gn[0][None], odd_conv_w[0])
    w_out = odd_w_out[0].astype(BF16)
    x2 = _out_proj(x2, yc.reshape(T, -1), yd.reshape(T, -1), w_out[:C_HEADS * C_VAL_DIM],
                   w_out[C_HEADS * C_VAL_DIM:])
    wr = jnp.pad(odd_router[0], ((0, 0), (0, LANE - N_EXPERTS)))
    route = _router(x2, odd_norm_ffn[0][None], wr)
    n_rows_pad = T * TOP_K + N_EXPERTS * tm
    pos, tile_expert, n_used, fill_rows = _routing(route, n_rows_pad)
    xs = _dispatch(pos, fill_rows, n_used, x2, n_rows_pad)
    ys = _ffn(tile_expert, n_used, xs, odd_norm_ffn[0][None],
              odd_we_gate[0].astype(BF16), odd_we_up[0].astype(BF16), odd_we_down[0].astype(BF16),
              residual=False)
    out = _combine(pos, x2, route, final_norm[None], ys)
    return out.reshape(B, S, D)
```

```python
import functools
import math

import numpy as np
import jax
import jax.numpy as jnp
from jax import lax
from jax.experimental import pallas as pl
from jax.experimental.pallas import tpu as pltpu

F32 = jnp.float32
BF16 = jnp.bfloat16

D_MODEL = 1024
CHUNK = 64
A_HEADS = 8
A_KV_HEADS = 2
A_HEAD_DIM = 64
A_GROUP = A_HEADS // A_KV_HEADS
WINDOW_CHUNKS = 2
B_HEADS = 8
B_Q_LORA = 384
B_KV_LORA = 256
B_NOPE = 64
B_ROPE = 32
B_V = 64
ROPE_THETA = 10000.0
C_HEADS = 4
C_KEY_DIM = 128
C_VAL_DIM = 128
D_CH = 512
CONV_W = 3
D_FF = 3584
N_EXPERTS = 8
TOP_K = 2
EPS = 1e-6
NEG_INF = -1e30
LOG2E = math.log2(math.e)

LANE = 128
ROW_TILE = 512
FF_TILE = 1792
ATT_Q = 128
MLA_T = 256
RET_BLOCK = 256
GATHER_ROWS = 256
VMEM_LIMIT = 56 * 1024 * 1024

_EV_QA = A_HEADS * LANE
_EV_KA = A_KV_HEADS * LANE
_EV_VA = A_KV_HEADS * LANE
_EV_OFF = np.cumsum([0, _EV_QA, _EV_KA, _EV_VA, _EV_VA, B_Q_LORA, B_KV_LORA, LANE, LANE])
_EV_COLS = int(_EV_OFF[-1])


def _cparams(semantics):
    return pltpu.CompilerParams(dimension_semantics=semantics, vmem_limit_bytes=VMEM_LIMIT)


def _rms(xf, g):
    return xf * lax.rsqrt(jnp.mean(xf * xf, axis=-1, keepdims=True) + EPS) * g


def _dot(a, b):
    return jnp.dot(a, b, preferred_element_type=F32)


def _dot_nt(a, b):
    return lax.dot_general(a, b, (((1,), (1,)), ((), ())), preferred_element_type=F32)


def _even_in_body(x_ref, pos_ref, g_ref, win_ref, invf_ref, qn_ref, wuqc_ref, wuqr_ref,
                  kvn_ref, wukc_ref, wuv_ref,
                  qa_ref, ka_ref, valo_ref, vahi_ref, qb_ref, kb_ref, vb_ref):
    h = _rms(x_ref[...], g_ref[...]).astype(BF16)

    def proj(k):
        return _dot(h, win_ref[:, int(_EV_OFF[k]):int(_EV_OFF[k + 1])])

    qa_ref[...] = (proj(0) * (A_HEAD_DIM ** -0.5 * LOG2E)).astype(BF16)
    ka_ref[...] = proj(1).astype(BF16)
    valo_ref[...] = proj(2).astype(BF16)
    vahi_ref[...] = proj(3).astype(BF16)

    ang = pos_ref[...].astype(F32) * invf_ref[...]
    lane = lax.broadcasted_iota(jnp.int32, ang.shape, 1)
    is_rope = (lane >= B_NOPE) & (lane < B_NOPE + B_ROPE)
    cosm = jnp.where(lane < B_NOPE, 1.0, jnp.where(is_rope, jnp.cos(ang), 0.0))
    sinm = jnp.where(is_rope, jnp.sin(ang), 0.0)

    cq = _rms(proj(4), qn_ref[...]).astype(BF16)
    q_main = _dot(cq, wuqc_ref[...])
    q_rot = _dot(cq, wuqr_ref[...])
    k_rope = proj(6) * cosm + proj(7) * sinm
    ckv = _rms(proj(5), kvn_ref[...]).astype(BF16)
    k_main = _dot(ckv, wukc_ref[...])
    scale = (B_NOPE + B_ROPE) ** -0.5 * LOG2E
    for hd in range(B_HEADS):
        sl = slice(hd * LANE, (hd + 1) * LANE)
        qb_ref[:, sl] = ((q_main[:, sl] * cosm + q_rot[:, sl] * sinm) * scale).astype(BF16)
        kb_ref[:, sl] = (k_main[:, sl] + k_rope).astype(BF16)
    vb_ref[...] = _dot(ckv, wuv_ref[...]).astype(BF16)


def _even_in(x2, pos2, g, w_in, inv_freq_row, q_norm, wuq_cat, wuq_rot, kv_norm, wuk_cat, wuv):
    T = x2.shape[0]
    tm = ROW_TILE
    row = lambda n: pl.BlockSpec((tm, n), lambda i: (i, 0))
    full = lambda a: pl.BlockSpec(a.shape, lambda i: (0, 0))
    outs = [_EV_QA, _EV_KA, _EV_VA, _EV_VA, B_HEADS * LANE, B_HEADS * LANE, B_HEADS * LANE]
    return pl.pallas_call(
        _even_in_body,
        grid=(T // tm,),
        in_specs=[row(D_MODEL), row(1), full(g), full(w_in), full(inv_freq_row), full(q_norm),
                  full(wuq_cat), full(wuq_rot), full(kv_norm), full(wuk_cat), full(wuv)],
        out_specs=[row(n) for n in outs],
        out_shape=[jax.ShapeDtypeStruct((T, n), BF16) for n in outs],
        compiler_params=_cparams(("parallel",)),
        name="even_in_proj",
    )(x2, pos2, g, w_in, inv_freq_row, q_norm, wuq_cat, wuq_rot, kv_norm, wuk_cat, wuv)


def _swa_body(slopes_ref, sinks_ref, q_ref, k_ref, vlo_ref, vhi_ref, o_ref):
    hk = pl.program_id(1)
    S = q_ref.shape[1]
    win = ATT_Q + WINDOW_CHUNKS * CHUNK
    slope = [slopes_ref[hk * A_GROUP + g] for g in range(A_GROUP)]
    sink = [sinks_ref[hk * A_GROUP + g] for g in range(A_GROUP)]

    def mask_bias(delta):
        qpos = delta + lax.broadcasted_iota(jnp.int32, (ATT_Q, win), 0)
        kpos = lax.broadcasted_iota(jnp.int32, (ATT_Q, win), 1)
        qc = qpos // CHUNK
        kc = kpos // CHUNK
        valid = (kc <= qc) & (kc >= qc - WINDOW_CHUNKS)
        dist = jnp.abs(qpos - kpos).astype(F32)
        return [jnp.where(valid, -slope[g] * dist, NEG_INF) for g in range(A_GROUP)]

    bias_by_delta = {0: mask_bias(0), WINDOW_CHUNKS * CHUNK: mask_bias(WINDOW_CHUNKS * CHUNK)}
    for i in range(S // ATT_Q):
        q0 = i * ATT_Q
        k0 = max(q0 - WINDOW_CHUNKS * CHUNK, 0)
        bias = bias_by_delta[q0 - k0]
        kw = k_ref[0, k0:k0 + win, :]
        vw = (vlo_ref[0, k0:k0 + win, :], vhi_ref[0, k0:k0 + win, :])
        for pair in range(A_GROUP // 2):
            o = jnp.zeros((ATT_Q, LANE), F32)
            for w in range(2):
                g = pair * 2 + w
                s = _dot_nt(q_ref[0, q0:q0 + ATT_Q, g * LANE:(g + 1) * LANE], kw) + bias[g]
                m = jnp.maximum(jnp.max(s, axis=-1, keepdims=True), sink[g])
                p = jnp.exp2(s - m)
                den = jnp.sum(p, axis=-1, keepdims=True) + jnp.exp2(sink[g] - m)
                o = o + _dot(p.astype(BF16), vw[w]) * (1.0 / den)
            o_ref[0, q0:q0 + ATT_Q, pair * LANE:(pair + 1) * LANE] = o.astype(BF16)


def _swa(slopes, sinks, qa, ka, valo, vahi):
    B, S, _ = qa.shape
    smem = pl.BlockSpec(memory_space=pltpu.SMEM)
    kv = pl.BlockSpec((1, S, LANE), lambda b, h: (b, 0, h))
    return pl.pallas_call(
        _swa_body,
        grid=(B, A_KV_HEADS),
        in_specs=[smem, smem, pl.BlockSpec((1, S, A_GROUP * LANE), lambda b, h: (b, 0, h)), kv, kv, kv],
        out_specs=pl.BlockSpec((1, S, A_GROUP * A_HEAD_DIM), lambda b, h: (b, 0, h)),
        out_shape=jax.ShapeDtypeStruct((B, S, A_HEADS * A_HEAD_DIM), BF16),
        compiler_params=_cparams(("parallel", "parallel")),
        name="swa_attention",
    )(slopes, sinks, qa, ka, valo, vahi)


def _mla_body(q_ref, k_ref, v_ref, o_ref):
    S = q_ref.shape[1]
    T = MLA_T
    row = lax.broadcasted_iota(jnp.int32, (T, T), 0) // CHUNK
    col = lax.broadcasted_iota(jnp.int32, (T, T), 1) // CHUNK
    diag_ok = col <= row

    for i in range(S // T):
        q0 = i * T
        out = jnp.zeros((T, LANE), F32)
        for hh in range(2):
            sl = slice(hh * LANE, (hh + 1) * LANE)
            q = q_ref[0, q0:q0 + T, sl]
            s_diag = jnp.where(diag_ok, _dot_nt(q, k_ref[0, q0:q0 + T, sl]), NEG_INF)
            m = jnp.max(s_diag, axis=-1, keepdims=True)
            if i > 0:
                s_past = _dot_nt(q, k_ref[0, 0:q0, sl])
                m = jnp.maximum(m, jnp.max(s_past, axis=-1, keepdims=True))
            p = jnp.exp2(s_diag - m)
            den = jnp.sum(p, axis=-1, keepdims=True)
            acc = _dot(p.astype(BF16), v_ref[0, q0:q0 + T, sl])
            if i > 0:
                p = jnp.exp2(s_past - m)
                den = den + jnp.sum(p, axis=-1, keepdims=True)
                acc = acc + _dot(p.astype(BF16), v_ref[0, 0:q0, sl])
            out = out + acc * (1.0 / den)
        o_ref[0, q0:q0 + T, :] = out.astype(BF16)


def _mla(qb, kb, vb):
    B, S, _ = qb.shape
    spec = pl.BlockSpec((1, S, 2 * LANE), lambda b, p: (b, 0, p))
    return pl.pallas_call(
        _mla_body,
        grid=(B, B_HEADS // 2),
        in_specs=[spec, spec, spec],
        out_specs=pl.BlockSpec((1, S, 2 * B_V), lambda b, p: (b, 0, p)),
        out_shape=jax.ShapeDtypeStruct((B, S, B_HEADS * B_V), BF16),
        compiler_params=_cparams(("parallel", "parallel")),
        name="mla_attention",
    )(qb, kb, vb)


def _out_proj_body(x_ref, a_ref, b_ref, wa_ref, wb_ref, o_ref):
    o_ref[...] = x_ref[...] + _dot(a_ref[...], wa_ref[...]) + _dot(b_ref[...], wb_ref[...])


def _out_proj(x2, a, b, wa, wb):
    T = x2.shape[0]
    tm = ROW_TILE
    row = lambda n: pl.BlockSpec((tm, n), lambda i: (i, 0))
    full = lambda w: pl.BlockSpec(w.shape, lambda i: (0, 0))
    return pl.pallas_call(
        _out_proj_body,
        grid=(T // tm,),
        in_specs=[row(D_MODEL), row(a.shape[1]), row(b.shape[1]), full(wa), full(wb)],
        out_specs=row(D_MODEL),
        out_shape=jax.ShapeDtypeStruct((T, D_MODEL), F32),
        compiler_params=_cparams(("parallel",)),
        name="out_proj",
    )(x2, a, b, wa, wb)


def _ffn_body(te_ref, nu_ref, x_ref, g_ref, wg_ref, wu_ref, wd_ref, o_ref, h_scr, acc_scr, *,
              residual):
    i = pl.program_id(0)
    j = pl.program_id(1)

    @pl.when(i < nu_ref[0])
    def _():
        @pl.when(j == 0)
        def _():
            xf = x_ref[...]
            h_scr[...] = _rms(xf, g_ref[...]).astype(BF16)
            acc_scr[...] = xf if residual else jnp.zeros_like(xf)

        h = h_scr[...]
        gate = _dot(h, wg_ref[0])
        up = _dot(h, wu_ref[0])
        hid = (gate * jax.nn.sigmoid(gate) * up).astype(BF16)
        acc_scr[...] += _dot(hid, wd_ref[0])

        @pl.when(j == pl.num_programs(1) - 1)
        def _():
            o_ref[...] = acc_scr[...]

    @pl.when(i >= nu_ref[0])
    def _():
        o_ref[...] = jnp.zeros_like(o_ref)


def _ffn(tile_expert, n_used, xs, g, wg, wu, wd, *, residual):
    P = xs.shape[0]
    tm, tf = ROW_TILE, FF_TILE
    nj = D_FF // tf

    def tile(i, nu):
        return jnp.minimum(i, nu[0] - 1)

    def ffcol(i, j, nu):
        return jnp.where(i < nu[0], j, nj - 1)

    grid_spec = pltpu.PrefetchScalarGridSpec(
        num_scalar_prefetch=2,
        grid=(P // tm, nj),
        in_specs=[
            pl.BlockSpec((tm, D_MODEL), lambda i, j, te, nu: (tile(i, nu), 0)),
            pl.BlockSpec((1, D_MODEL), lambda i, j, te, nu: (0, 0)),
            pl.BlockSpec((1, D_MODEL, tf), lambda i, j, te, nu: (te[tile(i, nu)], 0, ffcol(i, j, nu))),
            pl.BlockSpec((1, D_MODEL, tf), lambda i, j, te, nu: (te[tile(i, nu)], 0, ffcol(i, j, nu))),
            pl.BlockSpec((1, tf, D_MODEL), lambda i, j, te, nu: (te[tile(i, nu)], ffcol(i, j, nu), 0)),
        ],
        out_specs=pl.BlockSpec((tm, D_MODEL), lambda i, j, te, nu: (i, 0)),
        scratch_shapes=[pltpu.VMEM((tm, D_MODEL), BF16), pltpu.VMEM((tm, D_MODEL), F32)],
    )
    return pl.pallas_call(
        functools.partial(_ffn_body, residual=residual),
        grid_spec=grid_spec,
        out_shape=jax.ShapeDtypeStruct((P, D_MODEL), F32),
        compiler_params=_cparams(("arbitrary", "arbitrary")),
        name="swiglu_residual" if residual else "swiglu_experts",
    )(tile_expert, n_used, xs, g, wg, wu, wd)


def _odd_in_body(x_ref, g_ref, w_ref, o_ref):
    h = _rms(x_ref[...], g_ref[...]).astype(BF16)
    n = o_ref.shape[1]
    step = 512
    for c in range(n // step):
        sl = slice(c * step, (c + 1) * step)
        o_ref[:, sl] = _dot(h, w_ref[:, sl]).astype(BF16)


def _odd_in(x2, g, w):
    T = x2.shape[0]
    tm = ROW_TILE
    n = w.shape[1]
    return pl.pallas_call(
        _odd_in_body,
        grid=(T // tm,),
        in_specs=[pl.BlockSpec((tm, D_MODEL), lambda i: (i, 0)),
                  pl.BlockSpec(g.shape, lambda i: (0, 0)),
                  pl.BlockSpec(w.shape, lambda i: (0, 0))],
        out_specs=pl.BlockSpec((tm, n), lambda i: (i, 0)),
        out_shape=jax.ShapeDtypeStruct((T, n), BF16),
        compiler_params=_cparams(("parallel",)),
        name="odd_in_proj",
    )(x2, g, w)


def _ret_conv_body(q_ref, k_ref, v_ref, gt_ref, bd_ref, cd_ref, hd_ref, gn_ref, cw_ref,
                   yc_ref, yd_ref, u_scr):
    S = q_ref.shape[1]
    C = RET_BLOCK
    dk = C_KEY_DIM
    scale = dk ** -0.5
    ri = lax.broadcasted_iota(jnp.int32, (C, C), 0)
    ci = lax.broadcasted_iota(jnp.int32, (C, C), 1)
    diff = (ri - ci).astype(F32)
    pos = lax.broadcasted_iota(jnp.int32, (C, 1), 0).astype(F32)
    for hd in range(C_HEADS):
        log_g = math.log(1.0 - 2.0 ** (-5.0 - hd))
        intra = jnp.where(diff >= 0, jnp.exp(log_g * jnp.maximum(diff, 0.0)), 0.0) * scale
        cross = jnp.exp(log_g * (pos + 1.0))
        state = jnp.exp(log_g * (C - 1.0 - pos)) * scale
        block_decay = math.exp(log_g * C)
        sl = slice(hd * dk, (hd + 1) * dk)
        R = None
        for n in range(S // C):
            rows = slice(n * C, (n + 1) * C)
            q = q_ref[0, rows, sl]
            k = k_ref[0, rows, sl]
            v = v_ref[0, rows, sl]
            scores = _dot_nt(q, k) * intra
            y = _dot(scores.astype(BF16), v)
            kd_t = (k.astype(F32) * state).T.astype(BF16)
            kv = _dot(kd_t, v)
            if R is None:
                R = kv
            else:
                y = y + _dot(q, R.astype(BF16)) * cross
                R = block_decay * R + kv
            mu = jnp.mean(y, axis=-1, keepdims=True)
            yc = y - mu
            var = jnp.mean(yc * yc, axis=-1, keepdims=True)
            yn = yc * lax.rsqrt(var + EPS) * gn_ref[:, sl]
            gate = gt_ref[0, rows, sl].astype(F32)
            yc_ref[0, rows, sl] = (gate * jax.nn.sigmoid(gate) * yn).astype(BF16)

    halo = 8
    u_scr[0:halo, :] = jnp.zeros((halo, D_CH), F32)
    blk = 256
    for r in range(S // blk):
        rows = slice(r * blk, (r + 1) * blk)
        u_scr[halo + r * blk:halo + (r + 1) * blk, :] = (
            cd_ref[0, rows, :].astype(F32) * hd_ref[0, rows, :].astype(F32))
    for r in range(S // blk):
        acc = jnp.zeros((blk, D_CH), F32)
        for jw in range(CONV_W):
            lo = halo - (CONV_W - 1) + jw + r * blk
            acc = acc + cw_ref[jw:jw + 1, :] * u_scr[lo:lo + blk, :]
        rows = slice(r * blk, (r + 1) * blk)
        yd_ref[0, rows, :] = (bd_ref[0, rows, :].astype(F32) * acc).astype(BF16)


def _ret_conv(z, gn, conv_w):
    B, S, _ = z.shape
    col = lambda c: pl.BlockSpec((1, S, 512), lambda b: (b, 0, c))
    full = lambda a: pl.BlockSpec(a.shape, lambda b: (0, 0))
    out = pl.BlockSpec((1, S, 512), lambda b: (b, 0, 0))
    return pl.pallas_call(
        _ret_conv_body,
        grid=(B,),
        in_specs=[col(c) for c in range(7)] + [full(gn), full(conv_w)],
        out_specs=[out, out],
        out_shape=[jax.ShapeDtypeStruct((B, S, 512), BF16)] * 2,
        scratch_shapes=[pltpu.VMEM((S + 8, D_CH), F32)],
        compiler_params=_cparams(("parallel",)),
        name="retention_conv",
    )(z, z, z, z, z, z, z, gn, conv_w)


def _out_proj_route_body(x_ref, a_ref, b_ref, wa_ref, wb_ref, g_ref, wr_ref, o_ref, route_ref):
    x = x_ref[...] + _dot(a_ref[...], wa_ref[...]) + _dot(b_ref[...], wb_ref[...])
    o_ref[...] = x
    h = _rms(x, g_ref[...])
    h_hi = h.astype(BF16)
    h_lo = (h - h_hi.astype(F32)).astype(BF16)
    hi_pass = _dot(h_hi, wr_ref[...])
    logits = hi_pass[:, :LANE] + hi_pass[:, LANE:] + _dot(h_lo, wr_ref[:, :LANE])
    lane = lax.broadcasted_iota(jnp.int32, logits.shape, 1)
    logits = jnp.where(lane < N_EXPERTS, logits, NEG_INF)
    m1 = jnp.max(logits, axis=-1, keepdims=True)
    i1 = jnp.min(jnp.where(logits == m1, lane, LANE), axis=-1, keepdims=True)
    rest = jnp.where(lane == i1, NEG_INF, logits)
    m2 = jnp.max(rest, axis=-1, keepdims=True)
    i2 = jnp.min(jnp.where(rest == m2, lane, LANE), axis=-1, keepdims=True)
    e2 = jnp.exp(m2 - m1)
    g1 = 1.0 / (1.0 + e2)
    g2 = e2 / (1.0 + e2)
    route_ref[...] = jnp.where(lane == 0, i1.astype(F32),
                               jnp.where(lane == 1, i2.astype(F32),
                                         jnp.where(lane == 2, g1, jnp.where(lane == 3, g2, 0.0))))


def _out_proj_route(x2, a, b, wa, wb, g, wr):
    T = x2.shape[0]
    tm = ROW_TILE
    row = lambda n: pl.BlockSpec((tm, n), lambda i: (i, 0))
    full = lambda w: pl.BlockSpec(w.shape, lambda i: (0, 0))
    return pl.pallas_call(
        _out_proj_route_body,
        grid=(T // tm,),
        in_specs=[row(D_MODEL), row(a.shape[1]), row(b.shape[1]), full(wa), full(wb), full(g), full(wr)],
        out_specs=[row(D_MODEL), row(LANE)],
        out_shape=[jax.ShapeDtypeStruct((T, D_MODEL), F32), jax.ShapeDtypeStruct((T, LANE), F32)],
        compiler_params=_cparams(("parallel",)),
        name="out_proj_router",
    )(x2, a, b, wa, wb, g, wr)


def _row_copy(src, dst, sem, src_row, dst_row):
    return pltpu.make_async_copy(src.at[pl.ds(src_row, 1)], dst.at[pl.ds(dst_row, 1)], sem)


def _dispatch_body(pos_ref, fill_ref, nu_ref, x_hbm, o_hbm, xbuf, zero_scr, load_sem, scat_sem,
                   zero_sem):
    i = pl.program_id(0)
    nb = pl.num_programs(0)
    nt = xbuf.shape[1]
    tm = ROW_TILE

    def zero_tile(row0):
        fill = pltpu.make_async_copy(zero_scr, o_hbm.at[pl.ds(pl.multiple_of(row0, tm), tm)], zero_sem)
        fill.start()
        fill.wait()

    @pl.when(i == 0)
    def _():
        zero_scr[...] = jnp.zeros_like(zero_scr)
        for e in range(N_EXPERTS):
            @pl.when(fill_ref[e] >= 0)
            def _():
                zero_tile(fill_ref[e])

        def tail(t, carry):
            zero_tile(t * tm)
            return carry

        lax.fori_loop(nu_ref[0], o_hbm.shape[0] // tm, tail, 0)

    def load(block, slot):
        return pltpu.make_async_copy(x_hbm.at[pl.ds(block * nt, nt)], xbuf.at[slot], load_sem.at[slot])

    def wait_scatter(slot):
        for _ in range(TOP_K):
            pltpu.make_async_copy(xbuf.at[slot], o_hbm.at[pl.ds(0, nt)], scat_sem.at[slot]).wait()

    slot = i % 3
    nxt = (i + 1) % 3

    @pl.when(i == 0)
    def _():
        load(0, 0).start()

    @pl.when(i >= 2)
    def _():
        wait_scatter(nxt)

    @pl.when(i + 1 < nb)
    def _():
        load(i + 1, nxt).start()

    load(i, slot).wait()
    base = i * nt * TOP_K

    def issue(r, carry):
        for k in range(TOP_K):
            _row_copy(xbuf.at[slot], o_hbm, scat_sem.at[slot], r, pos_ref[base + r * TOP_K + k]).start()
        return carry

    lax.fori_loop(0, nt, issue, 0, unroll=8)

    @pl.when(i == nb - 1)
    def _():
        @pl.when(i >= 1)
        def _():
            wait_scatter((i + 2) % 3)
        wait_scatter(slot)


def _dispatch(pos, fill_rows, n_used, x2, n_rows_pad):
    nt = GATHER_ROWS
    grid_spec = pltpu.PrefetchScalarGridSpec(
        num_scalar_prefetch=3,
        grid=(x2.shape[0] // nt,),
        in_specs=[pl.BlockSpec(memory_space=pl.ANY)],
        out_specs=pl.BlockSpec(memory_space=pl.ANY),
        scratch_shapes=[pltpu.VMEM((3, nt, D_MODEL), F32), pltpu.VMEM((ROW_TILE, D_MODEL), F32),
                        pltpu.SemaphoreType.DMA((3,)), pltpu.SemaphoreType.DMA((3,)),
                        pltpu.SemaphoreType.DMA(())],
    )
    return pl.pallas_call(
        _dispatch_body,
        grid_spec=grid_spec,
        out_shape=jax.ShapeDtypeStruct((n_rows_pad, D_MODEL), F32),
        compiler_params=_cparams(("arbitrary",)),
        name="dispatch_rows",
    )(pos, fill_rows, n_used, x2)


def _combine_body(pos_ref, x_ref, route_ref, g_ref, ys_hbm, o_ref, buf, sem):
    i = pl.program_id(0)
    n = x_ref.shape[0]

    def fetch(block, slot):
        base = block * n

        def issue(r, carry):
            for k in range(TOP_K):
                _row_copy(ys_hbm, buf.at[slot, k], sem.at[slot],
                          pos_ref[(base + r) * TOP_K + k], r).start()
            return carry

        lax.fori_loop(0, n, issue, 0, unroll=8)

    slot = i % 2

    @pl.when(i == 0)
    def _():
        fetch(0, 0)

    @pl.when(i + 1 < pl.num_programs(0))
    def _():
        fetch(i + 1, 1 - slot)

    for k in range(TOP_K):
        pltpu.make_async_copy(ys_hbm.at[pl.ds(0, n)], buf.at[slot, k], sem.at[slot]).wait()
    route = route_ref[...]
    y = x_ref[...] + route[:, 2:3] * buf[slot, 0] + route[:, 3:4] * buf[slot, 1]
    o_ref[...] = _rms(y, g_ref[...])


def _combine(pos, x2, route, g, ys):
    T = x2.shape[0]
    n = GATHER_ROWS
    grid_spec = pltpu.PrefetchScalarGridSpec(
        num_scalar_prefetch=1,
        grid=(T // n,),
        in_specs=[pl.BlockSpec((n, D_MODEL), lambda i, p: (i, 0)),
                  pl.BlockSpec((n, LANE), lambda i, p: (i, 0)),
                  pl.BlockSpec((1, D_MODEL), lambda i, p: (0, 0)),
                  pl.BlockSpec(memory_space=pl.ANY)],
        out_specs=pl.BlockSpec((n, D_MODEL), lambda i, p: (i, 0)),
        scratch_shapes=[pltpu.VMEM((2, TOP_K, n, D_MODEL), F32), pltpu.SemaphoreType.DMA((2,))],
    )
    return pl.pallas_call(
        _combine_body,
        grid_spec=grid_spec,
        out_shape=jax.ShapeDtypeStruct((T, D_MODEL), F32),
        compiler_params=_cparams(("arbitrary",)),
        name="combine_norm",
    )(pos, x2, route, g, ys)


def _pad_heads(w, heads, width, offset=0):
    k = w.shape[0]
    w = w.reshape(k, heads, width)
    w = jnp.pad(w, ((0, 0), (0, 0), (offset, LANE - width - offset)))
    return w.reshape(k, heads * LANE)


def _rot_half(w):
    half = w.shape[-1] // 2
    return jnp.concatenate([-w[..., half:], w[..., :half]], axis=-1)


def _even_weights(w_in, w_uq, w_ukv):
    sizes = (A_HEADS * A_HEAD_DIM, A_KV_HEADS * A_HEAD_DIM, A_KV_HEADS * A_HEAD_DIM,
             B_Q_LORA, B_KV_LORA, B_ROPE)
    offs = np.cumsum((0,) + sizes)
    wqa, wka, wva, wcq, wckv, wkr = [w_in[:, offs[k]:offs[k + 1]] for k in range(6)]
    w_in_cat = jnp.concatenate([
        _pad_heads(wqa, A_HEADS, A_HEAD_DIM),
        _pad_heads(wka, A_KV_HEADS, A_HEAD_DIM),
        _pad_heads(wva, A_KV_HEADS, A_HEAD_DIM),
        _pad_heads(wva, A_KV_HEADS, A_HEAD_DIM, offset=A_HEAD_DIM),
        wcq, wckv,
        _pad_heads(wkr, 1, B_ROPE, offset=B_NOPE),
        _pad_heads(_rot_half(wkr), 1, B_ROPE, offset=B_NOPE),
    ], axis=1).astype(BF16)
    uq = w_uq.reshape(B_Q_LORA, B_HEADS, B_NOPE + B_ROPE)
    pad_tail = ((0, 0), (0, 0), (0, LANE - B_NOPE - B_ROPE))
    wuq_cat = jnp.pad(uq, pad_tail).reshape(B_Q_LORA, B_HEADS * LANE).astype(BF16)
    uq_rot = jnp.concatenate([jnp.zeros_like(uq[..., :B_NOPE]), _rot_half(uq[..., B_NOPE:])], axis=-1)
    wuq_rot = jnp.pad(uq_rot, pad_tail).reshape(B_Q_LORA, B_HEADS * LANE).astype(BF16)
    ukv = w_ukv.reshape(B_KV_LORA, B_HEADS, B_NOPE + B_V)
    wuk_cat = jnp.pad(ukv[..., :B_NOPE], ((0, 0), (0, 0), (0, LANE - B_NOPE)))
    wuk_cat = wuk_cat.reshape(B_KV_LORA, B_HEADS * LANE).astype(BF16)
    v = ukv[..., B_NOPE:]
    zeros = jnp.zeros_like(v)
    even = jnp.concatenate([v, zeros], axis=-1)
    odd = jnp.concatenate([zeros, v], axis=-1)
    is_even = (jnp.arange(B_HEADS) % 2 == 0)[None, :, None]
    wuv = jnp.where(is_even, even, odd).reshape(B_KV_LORA, B_HEADS * LANE).astype(BF16)
    return w_in_cat, wuq_cat, wuq_rot, wuk_cat, wuv


def _routing(route, n_rows_pad):
    tm = ROW_TILE
    e_flat = route[:, :TOP_K].astype(jnp.int32).reshape(-1)
    onehot = (e_flat[:, None] == jnp.arange(N_EXPERTS, dtype=jnp.int32)[None, :]).astype(jnp.int32)
    csum = jnp.cumsum(onehot, axis=0)
    rank = jnp.sum(csum * onehot, axis=1) - 1
    counts = csum[-1]
    tiles_e = (counts + tm - 1) // tm
    tiles_end = jnp.cumsum(tiles_e)
    row_off = (tiles_end - tiles_e) * tm
    pos = (jnp.sum(onehot * row_off[None, :], axis=1) + rank).astype(jnp.int32)
    n_tiles = n_rows_pad // tm
    tile_expert = jnp.sum(jnp.arange(n_tiles, dtype=jnp.int32)[:, None] >= tiles_end[None, :], axis=1)
    tile_expert = jnp.minimum(tile_expert, N_EXPERTS - 1).astype(jnp.int32)
    n_used = tiles_end[-1:].astype(jnp.int32)
    fill_rows = jnp.where(tiles_e > 0, (tiles_end - 1) * tm, -1).astype(jnp.int32)
    return pos, tile_expert, n_used, fill_rows


def kernel(x, positions, even_norm_mix, even_w_in, even_sinks, even_q_norm, even_w_uq, even_kv_norm, even_w_ukv, even_w_out, even_norm_ffn, even_w_gate, even_w_up, even_w_down, odd_norm_mix, odd_w_in, odd_ret_gn, odd_conv_w, odd_w_out, odd_norm_ffn, odd_router, odd_we_gate, odd_we_up, odd_we_down, final_norm):
    B, S, D = x.shape
    T = B * S
    tm = ROW_TILE
    x2 = x.reshape(T, D)
    pos2 = positions.reshape(T, 1)

    inv_freq = ROPE_THETA ** (-np.arange(0, B_ROPE, 2, dtype=np.float32) / B_ROPE)
    inv_row = np.zeros((1, LANE), np.float32)
    inv_row[0, B_NOPE:B_NOPE + B_ROPE // 2] = inv_freq
    inv_row[0, B_NOPE + B_ROPE // 2:B_NOPE + B_ROPE] = inv_freq
    slopes = 2.0 ** (-8.0 * (np.arange(A_HEADS, dtype=np.float32) + 1.0) / A_HEADS)
    slopes = jnp.asarray(slopes * LOG2E, F32)
    dense_tiles = jnp.zeros((T // tm,), jnp.int32)
    dense_used = jnp.full((1,), T // tm, jnp.int32)

    w_in_cat, wuq_cat, wuq_rot, wuk_cat, wuv = _even_weights(even_w_in[0], even_w_uq[0], even_w_ukv[0])
    qa, ka, valo, vahi, qb, kb, vb = _even_in(
        x2, pos2, even_norm_mix[0][None], w_in_cat, jnp.asarray(inv_row),
        even_q_norm[0][None], wuq_cat, wuq_rot, even_kv_norm[0][None], wuk_cat, wuv)
    r3 = lambda t: t.reshape(B, S, t.shape[-1])
    ya = _swa(slopes, even_sinks[0] * LOG2E, r3(qa), r3(ka), r3(valo), r3(vahi)).reshape(T, -1)
    yb = _mla(r3(qb), r3(kb), r3(vb)).reshape(T, -1)
    w_out = even_w_out[0].astype(BF16)
    x2 = _out_proj(x2, ya, yb, w_out[:ya.shape[1]], w_out[ya.shape[1]:])
    x2 = _ffn(dense_tiles, dense_used, x2, even_norm_ffn[0][None],
              even_w_gate.astype(BF16), even_w_up.astype(BF16), even_w_down.astype(BF16),
              residual=True)

    z = _odd_in(x2, odd_norm_mix[0][None], odd_w_in[0].astype(BF16))
    yc, yd = _ret_conv(r3(z), odd_ret_gn[0][None], odd_conv_w[0])
    w_out = odd_w_out[0].astype(BF16)
    wr = jnp.pad(odd_router[0], ((0, 0), (0, LANE - N_EXPERTS)))
    wr_hi = wr.astype(BF16)
    wr_lo = (wr - wr_hi.astype(F32)).astype(BF16)
    x2, route = _out_proj_route(x2, yc.reshape(T, -1), yd.reshape(T, -1), w_out[:C_HEADS * C_VAL_DIM],
                                w_out[C_HEADS * C_VAL_DIM:], odd_norm_ffn[0][None],
                                jnp.concatenate([wr_hi, wr_lo], axis=1))
    n_rows_pad = T * TOP_K + N_EXPERTS * tm
    pos, tile_expert, n_used, fill_rows = _routing(route, n_rows_pad)
    xs = _dispatch(pos, fill_rows, n_used, x2, n_rows_pad)
    ys = _ffn(tile_expert, n_used, xs, odd_norm_ffn[0][None],
              odd_we_gate[0].astype(BF16), odd_we_up[0].astype(BF16), odd_we_down[0].astype(BF16),
              residual=False)
    out = _combine(pos, x2, route, final_norm[None], ys)
    return out.reshape(B, S, D)
```

```python
import functools
import math

import numpy as np
import jax
import jax.numpy as jnp
from jax import lax
from jax.experimental import pallas as pl
from jax.experimental.pallas import tpu as pltpu

F32 = jnp.float32
BF16 = jnp.bfloat16

D_MODEL = 1024
CHUNK = 64
A_HEADS = 8
A_KV_HEADS = 2
A_HEAD_DIM = 64
A_GROUP = A_HEADS // A_KV_HEADS
WINDOW_CHUNKS = 2
B_HEADS = 8
B_Q_LORA = 384
B_KV_LORA = 256
B_NOPE = 64
B_ROPE = 32
B_V = 64
ROPE_THETA = 10000.0
C_HEADS = 4
C_KEY_DIM = 128
C_VAL_DIM = 128
D_CH = 512
CONV_W = 3
D_FF = 3584
N_EXPERTS = 8
TOP_K = 2
EPS = 1e-6
NEG_INF = -1e30
LOG2E = math.log2(math.e)

LANE = 128
ROW_TILE = 512
FF_TILE = 1792
ATT_Q = 128
MLA_T = 256
RET_BLOCK = 256
GATHER_ROWS = 256
VMEM_LIMIT = 56 * 1024 * 1024

_EV_OFF = np.cumsum([0, A_HEADS * A_HEAD_DIM, A_KV_HEADS * A_HEAD_DIM, A_KV_HEADS * A_HEAD_DIM,
                     B_Q_LORA, B_KV_LORA, LANE])


def _cparams(semantics):
    return pltpu.CompilerParams(dimension_semantics=semantics, vmem_limit_bytes=VMEM_LIMIT)


def _rms(xf, g):
    return xf * lax.rsqrt(jnp.mean(xf * xf, axis=-1, keepdims=True) + EPS) * g


def _dot(a, b):
    return jnp.dot(a, b, preferred_element_type=F32)


def _dot_nt(a, b):
    return lax.dot_general(a, b, (((1,), (1,)), ((), ())), preferred_element_type=F32)


def _even_in_body(x_ref, pos_ref, g_ref, win_ref, invf_ref, qn_ref, wuq_ref, kvn_ref, wuk_ref, wuv_ref,
                  qa_ref, ka_ref, valo_ref, vahi_ref, qb_ref, kb_ref, vb_ref):
    tm = x_ref.shape[0]
    h = _rms(x_ref[...], g_ref[...]).astype(BF16)
    lane = lax.broadcasted_iota(jnp.int32, (tm, LANE), 1)
    lo = lane < LANE // 2

    def proj(k):
        return _dot(h, win_ref[:, int(_EV_OFF[k]):int(_EV_OFF[k + 1])])

    def swap_halves(t):
        return pltpu.roll(t, LANE // 2, axis=1)

    def split_pair(t, out_ref, first_tile):
        out_ref[:, first_tile * LANE:(first_tile + 1) * LANE] = jnp.where(lo, t, 0.0).astype(BF16)
        out_ref[:, (first_tile + 1) * LANE:(first_tile + 2) * LANE] = (
            jnp.where(lo, swap_halves(t), 0.0).astype(BF16))

    qa = proj(0) * (A_HEAD_DIM ** -0.5 * LOG2E)
    for pair in range(A_HEADS // 2):
        split_pair(qa[:, pair * LANE:(pair + 1) * LANE], qa_ref, 2 * pair)
    split_pair(proj(1), ka_ref, 0)
    va = proj(2)
    va_swapped = swap_halves(va)
    valo_ref[:, :LANE] = jnp.where(lo, va, 0.0).astype(BF16)
    valo_ref[:, LANE:] = jnp.where(lo, va_swapped, 0.0).astype(BF16)
    vahi_ref[:, :LANE] = jnp.where(lo, 0.0, va_swapped).astype(BF16)
    vahi_ref[:, LANE:] = jnp.where(lo, 0.0, va).astype(BF16)

    ang = pos_ref[...].astype(F32) * invf_ref[...]
    is_rope = (lane >= B_NOPE) & (lane < B_NOPE + B_ROPE)
    cosm = jnp.where(lane < B_NOPE, 1.0, jnp.where(is_rope, jnp.cos(ang), 0.0))
    sinm = jnp.where(is_rope, jnp.sin(ang), 0.0)
    first_half = lane < B_NOPE + B_ROPE // 2

    def rope(t):
        rot = jnp.where(first_half, -pltpu.roll(t, LANE - B_ROPE // 2, axis=1),
                        pltpu.roll(t, B_ROPE // 2, axis=1))
        return t * cosm + rot * sinm

    cq = _rms(proj(3), qn_ref[...]).astype(BF16)
    q_all = _dot(cq, wuq_ref[...])
    scale = (B_NOPE + B_ROPE) ** -0.5 * LOG2E
    for hd in range(B_HEADS):
        sl = slice(hd * LANE, (hd + 1) * LANE)
        qb_ref[:, sl] = (rope(q_all[:, sl]) * scale).astype(BF16)

    k_rope = rope(proj(5))
    ckv = _rms(proj(4), kvn_ref[...]).astype(BF16)
    k_all = _dot(ckv, wuk_ref[...])
    v_all = _dot(ckv, wuv_ref[...])
    for pair in range(B_HEADS // 2):
        sl = slice(pair * LANE, (pair + 1) * LANE)
        even = slice(2 * pair * LANE, (2 * pair + 1) * LANE)
        odd = slice((2 * pair + 1) * LANE, (2 * pair + 2) * LANE)
        kb_ref[:, even] = (jnp.where(lo, k_all[:, sl], 0.0) + k_rope).astype(BF16)
        kb_ref[:, odd] = (jnp.where(lo, swap_halves(k_all[:, sl]), 0.0) + k_rope).astype(BF16)
        vb_ref[:, even] = jnp.where(lo, v_all[:, sl], 0.0).astype(BF16)
        vb_ref[:, odd] = jnp.where(lo, 0.0, v_all[:, sl]).astype(BF16)


def _even_in(x2, pos2, g, w_in, inv_freq_row, q_norm, wuq, kv_norm, wuk, wuv):
    T = x2.shape[0]
    tm = ROW_TILE
    row = lambda n: pl.BlockSpec((tm, n), lambda i: (i, 0))
    full = lambda a: pl.BlockSpec(a.shape, lambda i: (0, 0))
    outs = [A_HEADS * LANE, A_KV_HEADS * LANE, A_KV_HEADS * LANE, A_KV_HEADS * LANE,
            B_HEADS * LANE, B_HEADS * LANE, B_HEADS * LANE]
    return pl.pallas_call(
        _even_in_body,
        grid=(T // tm,),
        in_specs=[row(D_MODEL), row(1), full(g), full(w_in), full(inv_freq_row), full(q_norm),
                  full(wuq), full(kv_norm), full(wuk), full(wuv)],
        out_specs=[row(n) for n in outs],
        out_shape=[jax.ShapeDtypeStruct((T, n), BF16) for n in outs],
        compiler_params=_cparams(("parallel",)),
        name="even_in_proj",
    )(x2, pos2, g, w_in, inv_freq_row, q_norm, wuq, kv_norm, wuk, wuv)


def _swa_body(slopes_ref, sinks_ref, q_ref, k_ref, vlo_ref, vhi_ref, o_ref):
    hk = pl.program_id(1)
    S = q_ref.shape[1]
    win = ATT_Q + WINDOW_CHUNKS * CHUNK
    slope = [slopes_ref[hk * A_GROUP + g] for g in range(A_GROUP)]
    sink = [sinks_ref[hk * A_GROUP + g] for g in range(A_GROUP)]

    def mask_bias(delta):
        qpos = delta + lax.broadcasted_iota(jnp.int32, (ATT_Q, win), 0)
        kpos = lax.broadcasted_iota(jnp.int32, (ATT_Q, win), 1)
        qc = qpos // CHUNK
        kc = kpos // CHUNK
        valid = (kc <= qc) & (kc >= qc - WINDOW_CHUNKS)
        dist = jnp.abs(qpos - kpos).astype(F32)
        return [jnp.where(valid, -slope[g] * dist, NEG_INF) for g in range(A_GROUP)]

    bias_by_delta = {0: mask_bias(0), WINDOW_CHUNKS * CHUNK: mask_bias(WINDOW_CHUNKS * CHUNK)}
    for i in range(S // ATT_Q):
        q0 = i * ATT_Q
        k0 = max(q0 - WINDOW_CHUNKS * CHUNK, 0)
        bias = bias_by_delta[q0 - k0]
        kw = k_ref[0, k0:k0 + win, :]
        vw = (vlo_ref[0, k0:k0 + win, :], vhi_ref[0, k0:k0 + win, :])
        for pair in range(A_GROUP // 2):
            o = jnp.zeros((ATT_Q, LANE), F32)
            for w in range(2):
                g = pair * 2 + w
                s = _dot_nt(q_ref[0, q0:q0 + ATT_Q, g * LANE:(g + 1) * LANE], kw) + bias[g]
                m = jnp.maximum(jnp.max(s, axis=-1, keepdims=True), sink[g])
                p = jnp.exp2(s - m)
                den = jnp.sum(p, axis=-1, keepdims=True) + jnp.exp2(sink[g] - m)
                o = o + _dot(p.astype(BF16), vw[w]) * (1.0 / den)
            o_ref[0, q0:q0 + ATT_Q, pair * LANE:(pair + 1) * LANE] = o.astype(BF16)


def _swa(slopes, sinks, qa, ka, valo, vahi):
    B, S, _ = qa.shape
    smem = pl.BlockSpec(memory_space=pltpu.SMEM)
    kv = pl.BlockSpec((1, S, LANE), lambda b, h: (b, 0, h))
    return pl.pallas_call(
        _swa_body,
        grid=(B, A_KV_HEADS),
        in_specs=[smem, smem, pl.BlockSpec((1, S, A_GROUP * LANE), lambda b, h: (b, 0, h)), kv, kv, kv],
        out_specs=pl.BlockSpec((1, S, A_GROUP * A_HEAD_DIM), lambda b, h: (b, 0, h)),
        out_shape=jax.ShapeDtypeStruct((B, S, A_HEADS * A_HEAD_DIM), BF16),
        compiler_params=_cparams(("parallel", "parallel")),
        name="swa_attention",
    )(slopes, sinks, qa, ka, valo, vahi)


def _mla_body(q_ref, k_ref, v_ref, *rest):
    n_cast = (len(rest) - 1) // 2
    o_ref = rest[n_cast]
    for src, dst in zip(rest[:n_cast], rest[n_cast + 1:]):
        dst[...] = src[...].astype(BF16)
    S = q_ref.shape[1]
    T = MLA_T
    row = lax.broadcasted_iota(jnp.int32, (T, T), 0) // CHUNK
    col = lax.broadcasted_iota(jnp.int32, (T, T), 1) // CHUNK
    diag_ok = col <= row

    for i in range(S // T):
        q0 = i * T
        out = jnp.zeros((T, LANE), F32)
        for hh in range(2):
            sl = slice(hh * LANE, (hh + 1) * LANE)
            q = q_ref[0, q0:q0 + T, sl]
            s_diag = jnp.where(diag_ok, _dot_nt(q, k_ref[0, q0:q0 + T, sl]), NEG_INF)
            m = jnp.max(s_diag, axis=-1, keepdims=True)
            if i > 0:
                s_past = _dot_nt(q, k_ref[0, 0:q0, sl])
                m = jnp.maximum(m, jnp.max(s_past, axis=-1, keepdims=True))
            p = jnp.exp2(s_diag - m)
            den = jnp.sum(p, axis=-1, keepdims=True)
            acc = _dot(p.astype(BF16), v_ref[0, q0:q0 + T, sl])
            if i > 0:
                p = jnp.exp2(s_past - m)
                den = den + jnp.sum(p, axis=-1, keepdims=True)
                acc = acc + _dot(p.astype(BF16), v_ref[0, 0:q0, sl])
            out = out + acc * (1.0 / den)
        o_ref[0, q0:q0 + T, :] = out.astype(BF16)


def _mla(qb, kb, vb, f32_weights):
    B, S, _ = qb.shape
    n_pairs = B_HEADS // 2
    steps = B * n_pairs
    spec = pl.BlockSpec((1, S, 2 * LANE), lambda b, p: (b, 0, p))
    slab = lambda w: pl.BlockSpec((w.shape[0] // steps, w.shape[1]), lambda b, p: (b * n_pairs + p, 0))
    outs = pl.pallas_call(
        _mla_body,
        grid=(B, n_pairs),
        in_specs=[spec, spec, spec] + [slab(w) for w in f32_weights],
        out_specs=[pl.BlockSpec((1, S, 2 * B_V), lambda b, p: (b, 0, p))] + [slab(w) for w in f32_weights],
        out_shape=[jax.ShapeDtypeStruct((B, S, B_HEADS * B_V), BF16)]
        + [jax.ShapeDtypeStruct(w.shape, BF16) for w in f32_weights],
        compiler_params=_cparams(("parallel", "parallel")),
        name="mla_attention",
    )(qb, kb, vb, *f32_weights)
    return outs[0], outs[1:]


def _out_proj_body(x_ref, a_ref, b_ref, wa_ref, wb_ref, o_ref):
    o_ref[...] = x_ref[...] + _dot(a_ref[...], wa_ref[...]) + _dot(b_ref[...], wb_ref[...])


def _out_proj(x2, a, b, wa, wb):
    T = x2.shape[0]
    tm = ROW_TILE
    row = lambda n: pl.BlockSpec((tm, n), lambda i: (i, 0))
    full = lambda w: pl.BlockSpec(w.shape, lambda i: (0, 0))
    return pl.pallas_call(
        _out_proj_body,
        grid=(T // tm,),
        in_specs=[row(D_MODEL), row(a.shape[1]), row(b.shape[1]), full(wa), full(wb)],
        out_specs=row(D_MODEL),
        out_shape=jax.ShapeDtypeStruct((T, D_MODEL), F32),
        compiler_params=_cparams(("parallel",)),
        name="out_proj",
    )(x2, a, b, wa, wb)


def _ffn_body(te_ref, nu_ref, x_ref, g_ref, wg_ref, wu_ref, wd_ref, o_ref, h_scr, acc_scr, *,
              residual):
    i = pl.program_id(0)
    j = pl.program_id(1)

    @pl.when(i < nu_ref[0])
    def _():
        @pl.when(j == 0)
        def _():
            xf = x_ref[...]
            h_scr[...] = _rms(xf, g_ref[...]).astype(BF16)
            acc_scr[...] = xf if residual else jnp.zeros_like(xf)

        h = h_scr[...]
        gate = _dot(h, wg_ref[0])
        up = _dot(h, wu_ref[0])
        hid = (gate * jax.nn.sigmoid(gate) * up).astype(BF16)
        acc_scr[...] += _dot(hid, wd_ref[0])

        @pl.when(j == pl.num_programs(1) - 1)
        def _():
            o_ref[...] = acc_scr[...]

    @pl.when(i >= nu_ref[0])
    def _():
        o_ref[...] = jnp.zeros_like(o_ref)


def _ffn(tile_expert, n_used, xs, g, wg, wu, wd, *, residual):
    P = xs.shape[0]
    tm, tf = ROW_TILE, FF_TILE
    nj = D_FF // tf

    def tile(i, nu):
        return jnp.minimum(i, nu[0] - 1)

    def ffcol(i, j, nu):
        return jnp.where(i < nu[0], j, nj - 1)

    grid_spec = pltpu.PrefetchScalarGridSpec(
        num_scalar_prefetch=2,
        grid=(P // tm, nj),
        in_specs=[
            pl.BlockSpec((tm, D_MODEL), lambda i, j, te, nu: (tile(i, nu), 0)),
            pl.BlockSpec((1, D_MODEL), lambda i, j, te, nu: (0, 0)),
            pl.BlockSpec((1, D_MODEL, tf), lambda i, j, te, nu: (te[tile(i, nu)], 0, ffcol(i, j, nu))),
            pl.BlockSpec((1, D_MODEL, tf), lambda i, j, te, nu: (te[tile(i, nu)], 0, ffcol(i, j, nu))),
            pl.BlockSpec((1, tf, D_MODEL), lambda i, j, te, nu: (te[tile(i, nu)], ffcol(i, j, nu), 0)),
        ],
        out_specs=pl.BlockSpec((tm, D_MODEL), lambda i, j, te, nu: (i, 0)),
        scratch_shapes=[pltpu.VMEM((tm, D_MODEL), BF16), pltpu.VMEM((tm, D_MODEL), F32)],
    )
    return pl.pallas_call(
        functools.partial(_ffn_body, residual=residual),
        grid_spec=grid_spec,
        out_shape=jax.ShapeDtypeStruct((P, D_MODEL), F32),
        compiler_params=_cparams(("arbitrary", "arbitrary")),
        name="swiglu_residual" if residual else "swiglu_experts",
    )(tile_expert, n_used, xs, g, wg, wu, wd)


def _odd_in_body(x_ref, g_ref, w_ref, o_ref):
    h = _rms(x_ref[...], g_ref[...]).astype(BF16)
    n = o_ref.shape[1]
    step = 512
    for c in range(n // step):
        sl = slice(c * step, (c + 1) * step)
        o_ref[:, sl] = _dot(h, w_ref[:, sl]).astype(BF16)


def _odd_in(x2, g, w):
    T = x2.shape[0]
    tm = ROW_TILE
    n = w.shape[1]
    return pl.pallas_call(
        _odd_in_body,
        grid=(T // tm,),
        in_specs=[pl.BlockSpec((tm, D_MODEL), lambda i: (i, 0)),
                  pl.BlockSpec(g.shape, lambda i: (0, 0)),
                  pl.BlockSpec(w.shape, lambda i: (0, 0))],
        out_specs=pl.BlockSpec((tm, n), lambda i: (i, 0)),
        out_shape=jax.ShapeDtypeStruct((T, n), BF16),
        compiler_params=_cparams(("parallel",)),
        name="odd_in_proj",
    )(x2, g, w)


def _ret_conv_body(q_ref, k_ref, v_ref, gt_ref, bd_ref, cd_ref, hd_ref, gn_ref, cw_ref,
                   yc_ref, yd_ref, u_scr):
    S = q_ref.shape[1]
    C = RET_BLOCK
    dk = C_KEY_DIM
    scale = dk ** -0.5
    ri = lax.broadcasted_iota(jnp.int32, (C, C), 0)
    ci = lax.broadcasted_iota(jnp.int32, (C, C), 1)
    diff = (ri - ci).astype(F32)
    pos = lax.broadcasted_iota(jnp.int32, (C, 1), 0).astype(F32)
    for hd in range(C_HEADS):
        log_g = math.log(1.0 - 2.0 ** (-5.0 - hd))
        intra = jnp.where(diff >= 0, jnp.exp(log_g * jnp.maximum(diff, 0.0)), 0.0) * scale
        cross = jnp.exp(log_g * (pos + 1.0))
        state = jnp.exp(log_g * (C - 1.0 - pos)) * scale
        block_decay = math.exp(log_g * C)
        sl = slice(hd * dk, (hd + 1) * dk)
        R = None
        for n in range(S // C):
            rows = slice(n * C, (n + 1) * C)
            q = q_ref[0, rows, sl]
            k = k_ref[0, rows, sl]
            v = v_ref[0, rows, sl]
            scores = _dot_nt(q, k) * intra
            y = _dot(scores.astype(BF16), v)
            kd_t = (k.astype(F32) * state).T.astype(BF16)
            kv = _dot(kd_t, v)
            if R is None:
                R = kv
            else:
                y = y + _dot(q, R.astype(BF16)) * cross
                R = block_decay * R + kv
            mu = jnp.mean(y, axis=-1, keepdims=True)
            yc = y - mu
            var = jnp.mean(yc * yc, axis=-1, keepdims=True)
            yn = yc * lax.rsqrt(var + EPS) * gn_ref[:, sl]
            gate = gt_ref[0, rows, sl].astype(F32)
            yc_ref[0, rows, sl] = (gate * jax.nn.sigmoid(gate) * yn).astype(BF16)

    halo = 8
    u_scr[0:halo, :] = jnp.zeros((halo, D_CH), F32)
    blk = 256
    for r in range(S // blk):
        rows = slice(r * blk, (r + 1) * blk)
        u_scr[halo + r * blk:halo + (r + 1) * blk, :] = (
            cd_ref[0, rows, :].astype(F32) * hd_ref[0, rows, :].astype(F32))
    for r in range(S // blk):
        acc = jnp.zeros((blk, D_CH), F32)
        for jw in range(CONV_W):
            lo = halo - (CONV_W - 1) + jw + r * blk
            acc = acc + cw_ref[jw:jw + 1, :] * u_scr[lo:lo + blk, :]
        rows = slice(r * blk, (r + 1) * blk)
        yd_ref[0, rows, :] = (bd_ref[0, rows, :].astype(F32) * acc).astype(BF16)


def _ret_conv(z, gn, conv_w):
    B, S, _ = z.shape
    col = lambda c: pl.BlockSpec((1, S, 512), lambda b: (b, 0, c))
    full = lambda a: pl.BlockSpec(a.shape, lambda b: (0, 0))
    out = pl.BlockSpec((1, S, 512), lambda b: (b, 0, 0))
    return pl.pallas_call(
        _ret_conv_body,
        grid=(B,),
        in_specs=[col(c) for c in range(7)] + [full(gn), full(conv_w)],
        out_specs=[out, out],
        out_shape=[jax.ShapeDtypeStruct((B, S, 512), BF16)] * 2,
        scratch_shapes=[pltpu.VMEM((S + 8, D_CH), F32)],
        compiler_params=_cparams(("parallel",)),
        name="retention_conv",
    )(z, z, z, z, z, z, z, gn, conv_w)


def _out_proj_route_body(x_ref, a_ref, b_ref, wa_ref, wb_ref, g_ref, wr_ref, o_ref, route_ref):
    x = x_ref[...] + _dot(a_ref[...], wa_ref[...]) + _dot(b_ref[...], wb_ref[...])
    o_ref[...] = x
    h = _rms(x, g_ref[...])
    h_hi = h.astype(BF16)
    h_lo = (h - h_hi.astype(F32)).astype(BF16)
    hi_pass = _dot(h_hi, wr_ref[...])
    logits = hi_pass[:, :LANE] + hi_pass[:, LANE:] + _dot(h_lo, wr_ref[:, :LANE])
    lane = lax.broadcasted_iota(jnp.int32, logits.shape, 1)
    logits = jnp.where(lane < N_EXPERTS, logits, NEG_INF)
    m1 = jnp.max(logits, axis=-1, keepdims=True)
    i1 = jnp.min(jnp.where(logits == m1, lane, LANE), axis=-1, keepdims=True)
    rest = jnp.where(lane == i1, NEG_INF, logits)
    m2 = jnp.max(rest, axis=-1, keepdims=True)
    i2 = jnp.min(jnp.where(rest == m2, lane, LANE), axis=-1, keepdims=True)
    e2 = jnp.exp(m2 - m1)
    g1 = 1.0 / (1.0 + e2)
    g2 = e2 / (1.0 + e2)
    route_ref[...] = jnp.where(lane == 0, i1.astype(F32),
                               jnp.where(lane == 1, i2.astype(F32),
                                         jnp.where(lane == 2, g1, jnp.where(lane == 3, g2, 0.0))))


def _out_proj_route(x2, a, b, wa, wb, g, wr):
    T = x2.shape[0]
    tm = ROW_TILE
    row = lambda n: pl.BlockSpec((tm, n), lambda i: (i, 0))
    full = lambda w: pl.BlockSpec(w.shape, lambda i: (0, 0))
    return pl.pallas_call(
        _out_proj_route_body,
        grid=(T // tm,),
        in_specs=[row(D_MODEL), row(a.shape[1]), row(b.shape[1]), full(wa), full(wb), full(g), full(wr)],
        out_specs=[row(D_MODEL), row(LANE)],
        out_shape=[jax.ShapeDtypeStruct((T, D_MODEL), F32), jax.ShapeDtypeStruct((T, LANE), F32)],
        compiler_params=_cparams(("parallel",)),
        name="out_proj_router",
    )(x2, a, b, wa, wb, g, wr)


def _row_copy(src, dst, sem, src_row, dst_row):
    return pltpu.make_async_copy(src.at[pl.ds(src_row, 1)], dst.at[pl.ds(dst_row, 1)], sem)


def _dispatch_body(pos_ref, fill_ref, nu_ref, x_hbm, o_hbm, xbuf, zero_scr, load_sem, scat_sem,
                   zero_sem):
    i = pl.program_id(0)
    nb = pl.num_programs(0)
    nt = xbuf.shape[1]
    tm = ROW_TILE

    def zero_tile(row0):
        fill = pltpu.make_async_copy(zero_scr, o_hbm.at[pl.ds(pl.multiple_of(row0, tm), tm)], zero_sem)
        fill.start()
        fill.wait()

    @pl.when(i == 0)
    def _():
        zero_scr[...] = jnp.zeros_like(zero_scr)
        for e in range(N_EXPERTS):
            @pl.when(fill_ref[e] >= 0)
            def _():
                zero_tile(fill_ref[e])

        def tail(t, carry):
            zero_tile(t * tm)
            return carry

        lax.fori_loop(nu_ref[0], o_hbm.shape[0] // tm, tail, 0)

    def load(block, slot):
        return pltpu.make_async_copy(x_hbm.at[pl.ds(block * nt, nt)], xbuf.at[slot], load_sem.at[slot])

    def wait_scatter(slot):
        for _ in range(TOP_K):
            pltpu.make_async_copy(xbuf.at[slot], o_hbm.at[pl.ds(0, nt)], scat_sem.at[slot]).wait()

    slot = i % 3
    nxt = (i + 1) % 3

    @pl.when(i == 0)
    def _():
        load(0, 0).start()

    @pl.when(i >= 2)
    def _():
        wait_scatter(nxt)

    @pl.when(i + 1 < nb)
    def _():
        load(i + 1, nxt).start()

    load(i, slot).wait()
    base = i * nt * TOP_K

    def issue(r, carry):
        for k in range(TOP_K):
            _row_copy(xbuf.at[slot], o_hbm, scat_sem.at[slot], r, pos_ref[base + r * TOP_K + k]).start()
        return carry

    lax.fori_loop(0, nt, issue, 0, unroll=8)

    @pl.when(i == nb - 1)
    def _():
        @pl.when(i >= 1)
        def _():
            wait_scatter((i + 2) % 3)
        wait_scatter(slot)


def _dispatch(pos, fill_rows, n_used, x2, n_rows_pad):
    nt = GATHER_ROWS
    grid_spec = pltpu.PrefetchScalarGridSpec(
        num_scalar_prefetch=3,
        grid=(x2.shape[0] // nt,),
        in_specs=[pl.BlockSpec(memory_space=pl.ANY)],
        out_specs=pl.BlockSpec(memory_space=pl.ANY),
        scratch_shapes=[pltpu.VMEM((3, nt, D_MODEL), F32), pltpu.VMEM((ROW_TILE, D_MODEL), F32),
                        pltpu.SemaphoreType.DMA((3,)), pltpu.SemaphoreType.DMA((3,)),
                        pltpu.SemaphoreType.DMA(())],
    )
    return pl.pallas_call(
        _dispatch_body,
        grid_spec=grid_spec,
        out_shape=jax.ShapeDtypeStruct((n_rows_pad, D_MODEL), F32),
        compiler_params=_cparams(("arbitrary",)),
        name="dispatch_rows",
    )(pos, fill_rows, n_used, x2)


def _combine_body(pos_ref, x_ref, route_ref, g_ref, ys_hbm, o_ref, buf, sem):
    i = pl.program_id(0)
    n = x_ref.shape[0]

    def fetch(block, slot):
        base = block * n

        def issue(r, carry):
            for k in range(TOP_K):
                _row_copy(ys_hbm, buf.at[slot, k], sem.at[slot],
                          pos_ref[(base + r) * TOP_K + k], r).start()
            return carry

        lax.fori_loop(0, n, issue, 0, unroll=8)

    slot = i % 2

    @pl.when(i == 0)
    def _():
        fetch(0, 0)

    @pl.when(i + 1 < pl.num_programs(0))
    def _():
        fetch(i + 1, 1 - slot)

    for k in range(TOP_K):
        pltpu.make_async_copy(ys_hbm.at[pl.ds(0, n)], buf.at[slot, k], sem.at[slot]).wait()
    route = route_ref[...]
    y = x_ref[...] + route[:, 2:3] * buf[slot, 0] + route[:, 3:4] * buf[slot, 1]
    o_ref[...] = _rms(y, g_ref[...])


def _combine(pos, x2, route, g, ys):
    T = x2.shape[0]
    n = GATHER_ROWS
    grid_spec = pltpu.PrefetchScalarGridSpec(
        num_scalar_prefetch=1,
        grid=(T // n,),
        in_specs=[pl.BlockSpec((n, D_MODEL), lambda i, p: (i, 0)),
                  pl.BlockSpec((n, LANE), lambda i, p: (i, 0)),
                  pl.BlockSpec((1, D_MODEL), lambda i, p: (0, 0)),
                  pl.BlockSpec(memory_space=pl.ANY)],
        out_specs=pl.BlockSpec((n, D_MODEL), lambda i, p: (i, 0)),
        scratch_shapes=[pltpu.VMEM((2, TOP_K, n, D_MODEL), F32), pltpu.SemaphoreType.DMA((2,))],
    )
    return pl.pallas_call(
        _combine_body,
        grid_spec=grid_spec,
        out_shape=jax.ShapeDtypeStruct((T, D_MODEL), F32),
        compiler_params=_cparams(("arbitrary",)),
        name="combine_norm",
    )(pos, x2, route, g, ys)


def _even_weights(w_in, w_uq, w_ukv):
    n_main = w_in.shape[1] - B_ROPE
    w_kr = jnp.pad(w_in[:, n_main:], ((0, 0), (B_NOPE, LANE - B_NOPE - B_ROPE)))
    w_in_cat = jnp.concatenate([w_in[:, :n_main], w_kr], axis=1).astype(BF16)
    uq = w_uq.reshape(B_Q_LORA, B_HEADS, B_NOPE + B_ROPE)
    uq = jnp.pad(uq, ((0, 0), (0, 0), (0, LANE - B_NOPE - B_ROPE)))
    wuq = uq.reshape(B_Q_LORA, B_HEADS * LANE).astype(BF16)
    ukv = w_ukv.reshape(B_KV_LORA, B_HEADS, B_NOPE + B_V)
    wuk = ukv[..., :B_NOPE].reshape(B_KV_LORA, B_HEADS * B_NOPE).astype(BF16)
    wuv = ukv[..., B_NOPE:].reshape(B_KV_LORA, B_HEADS * B_V).astype(BF16)
    return w_in_cat, wuq, wuk, wuv


def _routing(route, n_rows_pad):
    tm = ROW_TILE
    e_flat = route[:, :TOP_K].astype(jnp.int32).reshape(-1)
    onehot = (e_flat[:, None] == jnp.arange(N_EXPERTS, dtype=jnp.int32)[None, :]).astype(jnp.int32)
    csum = jnp.cumsum(onehot, axis=0)
    rank = jnp.sum(csum * onehot, axis=1) - 1
    counts = csum[-1]
    tiles_e = (counts + tm - 1) // tm
    tiles_end = jnp.cumsum(tiles_e)
    row_off = (tiles_end - tiles_e) * tm
    pos = (jnp.sum(onehot * row_off[None, :], axis=1) + rank).astype(jnp.int32)
    n_tiles = n_rows_pad // tm
    tile_expert = jnp.sum(jnp.arange(n_tiles, dtype=jnp.int32)[:, None] >= tiles_end[None, :], axis=1)
    tile_expert = jnp.minimum(tile_expert, N_EXPERTS - 1).astype(jnp.int32)
    n_used = tiles_end[-1:].astype(jnp.int32)
    fill_rows = jnp.where(tiles_e > 0, (tiles_end - 1) * tm, -1).astype(jnp.int32)
    return pos, tile_expert, n_used, fill_rows


def kernel(x, positions, even_norm_mix, even_w_in, even_sinks, even_q_norm, even_w_uq, even_kv_norm, even_w_ukv, even_w_out, even_norm_ffn, even_w_gate, even_w_up, even_w_down, odd_norm_mix, odd_w_in, odd_ret_gn, odd_conv_w, odd_w_out, odd_norm_ffn, odd_router, odd_we_gate, odd_we_up, odd_we_down, final_norm):
    B, S, D = x.shape
    T = B * S
    tm = ROW_TILE
    x2 = x.reshape(T, D)
    pos2 = positions.reshape(T, 1)

    inv_freq = ROPE_THETA ** (-np.arange(0, B_ROPE, 2, dtype=np.float32) / B_ROPE)
    inv_row = np.zeros((1, LANE), np.float32)
    inv_row[0, B_NOPE:B_NOPE + B_ROPE // 2] = inv_freq
    inv_row[0, B_NOPE + B_ROPE // 2:B_NOPE + B_ROPE] = inv_freq
    slopes = 2.0 ** (-8.0 * (np.arange(A_HEADS, dtype=np.float32) + 1.0) / A_HEADS)
    slopes = jnp.asarray(slopes * LOG2E, F32)
    dense_tiles = jnp.zeros((T // tm,), jnp.int32)
    dense_used = jnp.full((1,), T // tm, jnp.int32)

    w_in_cat, wuq, wuk, wuv = _even_weights(even_w_in[0], even_w_uq[0], even_w_ukv[0])
    qa, ka, valo, vahi, qb, kb, vb = _even_in(
        x2, pos2, even_norm_mix[0][None], w_in_cat, jnp.asarray(inv_row),
        even_q_norm[0][None], wuq, even_kv_norm[0][None], wuk, wuv)
    r3 = lambda t: t.reshape(B, S, t.shape[-1])
    ya = _swa(slopes, even_sinks[0] * LOG2E, r3(qa), r3(ka), r3(valo), r3(vahi)).reshape(T, -1)
    expert_w = (odd_we_gate[0], odd_we_up[0], odd_we_down[0])
    yb, expert_w_bf16 = _mla(r3(qb), r3(kb), r3(vb), [w.reshape(-1, w.shape[-1]) for w in expert_w])
    yb = yb.reshape(T, -1)
    we_gate, we_up, we_down = [wb.reshape(w.shape) for wb, w in zip(expert_w_bf16, expert_w)]
    w_out = even_w_out[0].astype(BF16)
    x2 = _out_proj(x2, ya, yb, w_out[:ya.shape[1]], w_out[ya.shape[1]:])
    x2 = _ffn(dense_tiles, dense_used, x2, even_norm_ffn[0][None],
              even_w_gate.astype(BF16), even_w_up.astype(BF16), even_w_down.astype(BF16),
              residual=True)

    z = _odd_in(x2, odd_norm_mix[0][None], odd_w_in[0].astype(BF16))
    yc, yd = _ret_conv(r3(z), odd_ret_gn[0][None], odd_conv_w[0])
    w_out = odd_w_out[0].astype(BF16)
    wr = jnp.pad(odd_router[0], ((0, 0), (0, LANE - N_EXPERTS)))
    wr_hi = wr.astype(BF16)
    wr_lo = (wr - wr_hi.astype(F32)).astype(BF16)
    x2, route = _out_proj_route(x2, yc.reshape(T, -1), yd.reshape(T, -1), w_out[:C_HEADS * C_VAL_DIM],
                                w_out[C_HEADS * C_VAL_DIM:], odd_norm_ffn[0][None],
                                jnp.concatenate([wr_hi, wr_lo], axis=1))
    n_rows_pad = T * TOP_K + N_EXPERTS * tm
    pos, tile_expert, n_used, fill_rows = _routing(route, n_rows_pad)
    xs = _dispatch(pos, fill_rows, n_used, x2, n_rows_pad)
    ys = _ffn(tile_expert, n_used, xs, odd_norm_ffn[0][None],
              we_gate, we_up, we_down, residual=False)
    out = _combine(pos, x2, route, final_norm[None], ys)
    return out.reshape(B, S, D)
```

```python
import math

import numpy as np
import jax
import jax.numpy as jnp
from jax import lax
from jax.experimental import pallas as pl
from jax.experimental.pallas import tpu as pltpu

F32 = jnp.float32
BF16 = jnp.bfloat16

D_MODEL = 1024
CHUNK = 64
A_HEADS = 8
A_KV_HEADS = 2
A_HEAD_DIM = 64
A_GROUP = A_HEADS // A_KV_HEADS
WINDOW_CHUNKS = 2
B_HEADS = 8
B_Q_LORA = 384
B_KV_LORA = 256
B_NOPE = 64
B_ROPE = 32
B_V = 64
ROPE_THETA = 10000.0
C_HEADS = 4
C_KEY_DIM = 128
C_VAL_DIM = 128
D_CH = 512
CONV_W = 3
D_FF = 3584
N_EXPERTS = 8
TOP_K = 2
EPS = 1e-6
NEG_INF = -1e30
LOG2E = math.log2(math.e)

LANE = 128
ROW_TILE = 512
FF_TILE = 1792
ATT_Q = 128
MLA_T = 256
RET_BLOCK = 256
GATHER_ROWS = 256
VMEM_LIMIT = 56 * 1024 * 1024

_EV_OFF = np.cumsum([0, A_HEADS * A_HEAD_DIM, A_KV_HEADS * A_HEAD_DIM, A_KV_HEADS * A_HEAD_DIM,
                     B_Q_LORA, B_KV_LORA, LANE])


def _cparams(semantics):
    return pltpu.CompilerParams(dimension_semantics=semantics, vmem_limit_bytes=VMEM_LIMIT)


def _rms(xf, g):
    return xf * lax.rsqrt(jnp.mean(xf * xf, axis=-1, keepdims=True) + EPS) * g


def _dot(a, b):
    return jnp.dot(a, b, preferred_element_type=F32)


def _dot_nt(a, b):
    return lax.dot_general(a, b, (((1,), (1,)), ((), ())), preferred_element_type=F32)


def _even_in_body(x_ref, pos_ref, g_ref, win_ref, invf_ref, qn_ref, wuq_ref, kvn_ref, wuk_ref, wuv_ref,
                  *rest):
    n_cast = (len(rest) - 7) // 2
    qa_ref, ka_ref, valo_ref, vahi_ref, qb_ref, kb_ref, vb_ref = rest[n_cast:n_cast + 7]
    for src, dst in zip(rest[:n_cast], rest[n_cast + 7:]):
        dst[...] = src[...].astype(BF16)
    tm = x_ref.shape[0]
    h = _rms(x_ref[...], g_ref[...]).astype(BF16)
    lane = lax.broadcasted_iota(jnp.int32, (tm, LANE), 1)
    lo = lane < LANE // 2

    def proj(k):
        return _dot(h, win_ref[:, int(_EV_OFF[k]):int(_EV_OFF[k + 1])])

    def swap_halves(t):
        return pltpu.roll(t, LANE // 2, axis=1)

    def split_pair(t, out_ref, first_tile):
        out_ref[:, first_tile * LANE:(first_tile + 1) * LANE] = jnp.where(lo, t, 0.0).astype(BF16)
        out_ref[:, (first_tile + 1) * LANE:(first_tile + 2) * LANE] = (
            jnp.where(lo, swap_halves(t), 0.0).astype(BF16))

    qa = proj(0) * (A_HEAD_DIM ** -0.5 * LOG2E)
    for pair in range(A_HEADS // 2):
        split_pair(qa[:, pair * LANE:(pair + 1) * LANE], qa_ref, 2 * pair)
    split_pair(proj(1), ka_ref, 0)
    va = proj(2)
    va_swapped = swap_halves(va)
    valo_ref[:, :LANE] = jnp.where(lo, va, 0.0).astype(BF16)
    valo_ref[:, LANE:] = jnp.where(lo, va_swapped, 0.0).astype(BF16)
    vahi_ref[:, :LANE] = jnp.where(lo, 0.0, va_swapped).astype(BF16)
    vahi_ref[:, LANE:] = jnp.where(lo, 0.0, va).astype(BF16)

    ang = pos_ref[...].astype(F32) * invf_ref[...]
    is_rope = (lane >= B_NOPE) & (lane < B_NOPE + B_ROPE)
    cosm = jnp.where(lane < B_NOPE, 1.0, jnp.where(is_rope, jnp.cos(ang), 0.0))
    sinm = jnp.where(is_rope, jnp.sin(ang), 0.0)
    first_half = lane < B_NOPE + B_ROPE // 2

    def rope(t):
        rot = jnp.where(first_half, -pltpu.roll(t, LANE - B_ROPE // 2, axis=1),
                        pltpu.roll(t, B_ROPE // 2, axis=1))
        return t * cosm + rot * sinm

    cq = _rms(proj(3), qn_ref[...]).astype(BF16)
    q_all = _dot(cq, wuq_ref[...])
    scale = (B_NOPE + B_ROPE) ** -0.5 * LOG2E
    for hd in range(B_HEADS):
        sl = slice(hd * LANE, (hd + 1) * LANE)
        qb_ref[:, sl] = (rope(q_all[:, sl]) * scale).astype(BF16)

    k_rope = rope(proj(5))
    ckv = _rms(proj(4), kvn_ref[...]).astype(BF16)
    k_all = _dot(ckv, wuk_ref[...])
    v_all = _dot(ckv, wuv_ref[...])
    ones_lo = jnp.where(lane == LANE // 2, 1.0, 0.0)
    ones_hi = jnp.where(lane == 0, 1.0, 0.0)
    for pair in range(B_HEADS // 2):
        sl = slice(pair * LANE, (pair + 1) * LANE)
        even = slice(2 * pair * LANE, (2 * pair + 1) * LANE)
        odd = slice((2 * pair + 1) * LANE, (2 * pair + 2) * LANE)
        kb_ref[:, even] = (jnp.where(lo, k_all[:, sl], 0.0) + k_rope).astype(BF16)
        kb_ref[:, odd] = (jnp.where(lo, swap_halves(k_all[:, sl]), 0.0) + k_rope).astype(BF16)
        vb_ref[:, even] = jnp.where(lo, v_all[:, sl], ones_lo).astype(BF16)
        vb_ref[:, odd] = jnp.where(lo, ones_hi, v_all[:, sl]).astype(BF16)


def _even_in(x2, pos2, g, w_in, inv_freq_row, q_norm, wuq, kv_norm, wuk, wuv, f32_weights):
    T = x2.shape[0]
    tm = ROW_TILE
    steps = T // tm
    row = lambda n: pl.BlockSpec((tm, n), lambda i: (i, 0))
    full = lambda a: pl.BlockSpec(a.shape, lambda i: (0, 0))
    slab = lambda w: pl.BlockSpec((w.shape[0] // steps, w.shape[1]), lambda i: (i, 0))
    outs = [A_HEADS * LANE, A_KV_HEADS * LANE, A_KV_HEADS * LANE, A_KV_HEADS * LANE,
            B_HEADS * LANE, B_HEADS * LANE, B_HEADS * LANE]
    res = pl.pallas_call(
        _even_in_body,
        grid=(steps,),
        in_specs=[row(D_MODEL), row(1), full(g), full(w_in), full(inv_freq_row), full(q_norm),
                  full(wuq), full(kv_norm), full(wuk), full(wuv)] + [slab(w) for w in f32_weights],
        out_specs=[row(n) for n in outs] + [slab(w) for w in f32_weights],
        out_shape=[jax.ShapeDtypeStruct((T, n), BF16) for n in outs]
        + [jax.ShapeDtypeStruct(w.shape, BF16) for w in f32_weights],
        compiler_params=_cparams(("parallel",)),
        name="even_in_proj",
    )(x2, pos2, g, w_in, inv_freq_row, q_norm, wuq, kv_norm, wuk, wuv, *f32_weights)
    return res[:7], res[7:]


def _swa_body(slopes_ref, sinks_ref, q_ref, k_ref, vlo_ref, vhi_ref, o_ref):
    hk = pl.program_id(1)
    S = q_ref.shape[1]
    win = ATT_Q + WINDOW_CHUNKS * CHUNK
    slope = [slopes_ref[hk * A_GROUP + g] for g in range(A_GROUP)]
    sink = [sinks_ref[hk * A_GROUP + g] for g in range(A_GROUP)]

    def mask_bias(delta):
        qpos = delta + lax.broadcasted_iota(jnp.int32, (ATT_Q, win), 0)
        kpos = lax.broadcasted_iota(jnp.int32, (ATT_Q, win), 1)
        qc = qpos // CHUNK
        kc = kpos // CHUNK
        valid = (kc <= qc) & (kc >= qc - WINDOW_CHUNKS)
        dist = jnp.abs(qpos - kpos).astype(F32)
        return [jnp.where(valid, -slope[g] * dist, NEG_INF) for g in range(A_GROUP)]

    bias_by_delta = {0: mask_bias(0), WINDOW_CHUNKS * CHUNK: mask_bias(WINDOW_CHUNKS * CHUNK)}
    for i in range(S // ATT_Q):
        q0 = i * ATT_Q
        k0 = max(q0 - WINDOW_CHUNKS * CHUNK, 0)
        bias = bias_by_delta[q0 - k0]
        kw = k_ref[0, k0:k0 + win, :]
        vw = (vlo_ref[0, k0:k0 + win, :], vhi_ref[0, k0:k0 + win, :])
        for pair in range(A_GROUP // 2):
            o = jnp.zeros((ATT_Q, LANE), F32)
            for w in range(2):
                g = pair * 2 + w
                s = _dot_nt(q_ref[0, q0:q0 + ATT_Q, g * LANE:(g + 1) * LANE], kw) + bias[g]
                m = jnp.maximum(jnp.max(s, axis=-1, keepdims=True), sink[g])
                p = jnp.exp2(s - m)
                den = jnp.sum(p, axis=-1, keepdims=True) + jnp.exp2(sink[g] - m)
                o = o + _dot(p.astype(BF16), vw[w]) * (1.0 / den)
            o_ref[0, q0:q0 + ATT_Q, pair * LANE:(pair + 1) * LANE] = o.astype(BF16)


def _swa(slopes, sinks, qa, ka, valo, vahi):
    B, S, _ = qa.shape
    smem = pl.BlockSpec(memory_space=pltpu.SMEM)
    kv = pl.BlockSpec((1, S, LANE), lambda b, h: (b, 0, h))
    return pl.pallas_call(
        _swa_body,
        grid=(B, A_KV_HEADS),
        in_specs=[smem, smem, pl.BlockSpec((1, S, A_GROUP * LANE), lambda b, h: (b, 0, h)), kv, kv, kv],
        out_specs=pl.BlockSpec((1, S, A_GROUP * A_HEAD_DIM), lambda b, h: (b, 0, h)),
        out_shape=jax.ShapeDtypeStruct((B, S, A_HEADS * A_HEAD_DIM), BF16),
        compiler_params=_cparams(("parallel", "parallel")),
        name="swa_attention",
    )(slopes, sinks, qa, ka, valo, vahi)


def _mla_body(q_ref, k_ref, v_ref, *rest):
    n_cast = (len(rest) - 1) // 2
    o_ref = rest[n_cast]
    for src, dst in zip(rest[:n_cast], rest[n_cast + 1:]):
        dst[...] = src[...].astype(BF16)
    S = q_ref.shape[1]
    T = MLA_T
    row = lax.broadcasted_iota(jnp.int32, (T, T), 0) // CHUNK
    col = lax.broadcasted_iota(jnp.int32, (T, T), 1) // CHUNK
    diag_ok = col <= row
    lo = lax.broadcasted_iota(jnp.int32, (T, LANE), 1) < LANE // 2

    for i in range(S // T):
        q0 = i * T
        normed = []
        for hh in range(2):
            sl = slice(hh * LANE, (hh + 1) * LANE)
            q = q_ref[0, q0:q0 + T, sl]
            s_diag = jnp.where(diag_ok, _dot_nt(q, k_ref[0, q0:q0 + T, sl]), NEG_INF)
            m = jnp.max(s_diag, axis=-1, keepdims=True)
            if i > 0:
                s_past = _dot_nt(q, k_ref[0, 0:q0, sl])
                m = jnp.maximum(m, jnp.max(s_past, axis=-1, keepdims=True))
            acc = _dot(jnp.exp2(s_diag - m).astype(BF16), v_ref[0, q0:q0 + T, sl])
            if i > 0:
                acc = acc + _dot(jnp.exp2(s_past - m).astype(BF16), v_ref[0, 0:q0, sl])
            den_lane = LANE // 2 if hh == 0 else 0
            normed.append(acc * (1.0 / acc[:, den_lane:den_lane + 1]))
        o_ref[0, q0:q0 + T, :] = jnp.where(lo, normed[0], normed[1]).astype(BF16)


def _mla(qb, kb, vb, f32_weights):
    B, S, _ = qb.shape
    n_pairs = B_HEADS // 2
    steps = B * n_pairs
    spec = pl.BlockSpec((1, S, 2 * LANE), lambda b, p: (b, 0, p))
    slab = lambda w: pl.BlockSpec((w.shape[0] // steps, w.shape[1]), lambda b, p: (b * n_pairs + p, 0))
    outs = pl.pallas_call(
        _mla_body,
        grid=(B, n_pairs),
        in_specs=[spec, spec, spec] + [slab(w) for w in f32_weights],
        out_specs=[pl.BlockSpec((1, S, 2 * B_V), lambda b, p: (b, 0, p))] + [slab(w) for w in f32_weights],
        out_shape=[jax.ShapeDtypeStruct((B, S, B_HEADS * B_V), BF16)]
        + [jax.ShapeDtypeStruct(w.shape, BF16) for w in f32_weights],
        compiler_params=_cparams(("parallel", "parallel")),
        name="mla_attention",
    )(qb, kb, vb, *f32_weights)
    return outs[0], outs[1:]


def _swiglu_partial(h, wg, wu, wd):
    gate = _dot(h, wg)
    up = _dot(h, wu)
    return _dot((gate * jax.nn.sigmoid(gate) * up).astype(BF16), wd)


def _dense_ffn_body(x_ref, a_ref, b_ref, wa_ref, wb_ref, g_ref, wg_ref, wu_ref, wd_ref, o_ref, h_scr):
    @pl.when(pl.program_id(1) == 0)
    def _():
        x1 = x_ref[...] + _dot(a_ref[...], wa_ref[...]) + _dot(b_ref[...], wb_ref[...])
        h_scr[...] = _rms(x1, g_ref[...]).astype(BF16)
        o_ref[...] = x1

    o_ref[...] += _swiglu_partial(h_scr[...], wg_ref[...], wu_ref[...], wd_ref[...])


def _out_proj_dense_ffn(x2, a, b, wa, wb, g, wg, wu, wd):
    T = x2.shape[0]
    tm, tf = ROW_TILE, FF_TILE
    row = lambda n: pl.BlockSpec((tm, n), lambda i, j: (i, 0))
    full = lambda w: pl.BlockSpec(w.shape, lambda i, j: (0, 0))
    return pl.pallas_call(
        _dense_ffn_body,
        grid=(T // tm, D_FF // tf),
        in_specs=[row(D_MODEL), row(a.shape[1]), row(b.shape[1]), full(wa), full(wb), full(g),
                  pl.BlockSpec((D_MODEL, tf), lambda i, j: (0, j)),
                  pl.BlockSpec((D_MODEL, tf), lambda i, j: (0, j)),
                  pl.BlockSpec((tf, D_MODEL), lambda i, j: (j, 0))],
        out_specs=row(D_MODEL),
        out_shape=jax.ShapeDtypeStruct((T, D_MODEL), F32),
        scratch_shapes=[pltpu.VMEM((tm, D_MODEL), BF16)],
        compiler_params=_cparams(("parallel", "arbitrary")),
        name="out_proj_swiglu",
    )(x2, a, b, wa, wb, g, wg, wu, wd)


def _moe_ffn_body(te_ref, nu_ref, x_ref, g_ref, wg_ref, wu_ref, wd_ref, o_ref, h_scr):
    i = pl.program_id(0)
    j = pl.program_id(1)

    @pl.when(i < nu_ref[0])
    def _():
        @pl.when(j == 0)
        def _():
            h_scr[...] = _rms(x_ref[...], g_ref[...]).astype(BF16)

        part = _swiglu_partial(h_scr[...], wg_ref[0], wu_ref[0], wd_ref[0])

        @pl.when(j == 0)
        def _():
            o_ref[...] = part

        @pl.when(j > 0)
        def _():
            o_ref[...] += part

    @pl.when((i >= nu_ref[0]) & (j == 0))
    def _():
        o_ref[...] = jnp.zeros_like(o_ref)


def _moe_ffn(tile_expert, n_used, xs, g, wg, wu, wd):
    P = xs.shape[0]
    tm, tf = ROW_TILE, FF_TILE
    nj = D_FF // tf

    def tile(i, nu):
        return jnp.minimum(i, nu[0] - 1)

    def ffcol(i, j, nu):
        return jnp.where(i < nu[0], j, nj - 1)

    grid_spec = pltpu.PrefetchScalarGridSpec(
        num_scalar_prefetch=2,
        grid=(P // tm, nj),
        in_specs=[
            pl.BlockSpec((tm, D_MODEL), lambda i, j, te, nu: (tile(i, nu), 0)),
            pl.BlockSpec((1, D_MODEL), lambda i, j, te, nu: (0, 0)),
            pl.BlockSpec((1, D_MODEL, tf), lambda i, j, te, nu: (te[tile(i, nu)], 0, ffcol(i, j, nu))),
            pl.BlockSpec((1, D_MODEL, tf), lambda i, j, te, nu: (te[tile(i, nu)], 0, ffcol(i, j, nu))),
            pl.BlockSpec((1, tf, D_MODEL), lambda i, j, te, nu: (te[tile(i, nu)], ffcol(i, j, nu), 0)),
        ],
        out_specs=pl.BlockSpec((tm, D_MODEL), lambda i, j, te, nu: (i, 0)),
        scratch_shapes=[pltpu.VMEM((tm, D_MODEL), BF16)],
    )
    return pl.pallas_call(
        _moe_ffn_body,
        grid_spec=grid_spec,
        out_shape=jax.ShapeDtypeStruct((P, D_MODEL), F32),
        compiler_params=_cparams(("arbitrary", "arbitrary")),
        name="swiglu_experts",
    )(tile_expert, n_used, xs, g, wg, wu, wd)


def _odd_in_body(x_ref, g_ref, w_ref, o_ref):
    h = _rms(x_ref[...], g_ref[...]).astype(BF16)
    n = o_ref.shape[1]
    step = 512
    for c in range(n // step):
        sl = slice(c * step, (c + 1) * step)
        o_ref[:, sl] = _dot(h, w_ref[:, sl]).astype(BF16)


def _odd_in(x2, g, w):
    T = x2.shape[0]
    tm = ROW_TILE
    n = w.shape[1]
    return pl.pallas_call(
        _odd_in_body,
        grid=(T // tm,),
        in_specs=[pl.BlockSpec((tm, D_MODEL), lambda i: (i, 0)),
                  pl.BlockSpec(g.shape, lambda i: (0, 0)),
                  pl.BlockSpec(w.shape, lambda i: (0, 0))],
        out_specs=pl.BlockSpec((tm, n), lambda i: (i, 0)),
        out_shape=jax.ShapeDtypeStruct((T, n), BF16),
        compiler_params=_cparams(("parallel",)),
        name="odd_in_proj",
    )(x2, g, w)


def _ret_conv_body(q_ref, k_ref, v_ref, gt_ref, bd_ref, cd_ref, hd_ref, gn_ref, cw_ref,
                   yc_ref, yd_ref, u_scr):
    S = q_ref.shape[1]
    C = RET_BLOCK
    dk = C_KEY_DIM
    scale = dk ** -0.5
    ri = lax.broadcasted_iota(jnp.int32, (C, C), 0)
    ci = lax.broadcasted_iota(jnp.int32, (C, C), 1)
    diff = (ri - ci).astype(F32)
    pos = lax.broadcasted_iota(jnp.int32, (C, 1), 0).astype(F32)
    for hd in range(C_HEADS):
        log_g = math.log(1.0 - 2.0 ** (-5.0 - hd))
        intra = jnp.where(diff >= 0, jnp.exp(log_g * jnp.maximum(diff, 0.0)), 0.0) * scale
        cross = jnp.exp(log_g * (pos + 1.0))
        state = jnp.exp(log_g * (C - 1.0 - pos)) * scale
        block_decay = math.exp(log_g * C)
        sl = slice(hd * dk, (hd + 1) * dk)
        R = None
        for n in range(S // C):
            rows = slice(n * C, (n + 1) * C)
            q = q_ref[0, rows, sl]
            k = k_ref[0, rows, sl]
            v = v_ref[0, rows, sl]
            scores = _dot_nt(q, k) * intra
            y = _dot(scores.astype(BF16), v)
            kd_t = (k.astype(F32) * state).T.astype(BF16)
            kv = _dot(kd_t, v)
            if R is None:
                R = kv
            else:
                y = y + _dot(q, R.astype(BF16)) * cross
                R = block_decay * R + kv
            mu = jnp.mean(y, axis=-1, keepdims=True)
            yc = y - mu
            var = jnp.mean(yc * yc, axis=-1, keepdims=True)
            yn = yc * lax.rsqrt(var + EPS) * gn_ref[:, sl]
            gate = gt_ref[0, rows, sl].astype(F32)
            yc_ref[0, rows, sl] = (gate * jax.nn.sigmoid(gate) * yn).astype(BF16)

    halo = 8
    u_scr[0:halo, :] = jnp.zeros((halo, D_CH), F32)
    blk = 256
    for r in range(S // blk):
        rows = slice(r * blk, (r + 1) * blk)
        u_scr[halo + r * blk:halo + (r + 1) * blk, :] = (
            cd_ref[0, rows, :].astype(F32) * hd_ref[0, rows, :].astype(F32))
    for r in range(S // blk):
        acc = jnp.zeros((blk, D_CH), F32)
        for jw in range(CONV_W):
            lo = halo - (CONV_W - 1) + jw + r * blk
            acc = acc + cw_ref[jw:jw + 1, :] * u_scr[lo:lo + blk, :]
        rows = slice(r * blk, (r + 1) * blk)
        yd_ref[0, rows, :] = (bd_ref[0, rows, :].astype(F32) * acc).astype(BF16)


def _ret_conv(z, gn, conv_w):
    B, S, _ = z.shape
    col = lambda c: pl.BlockSpec((1, S, 512), lambda b: (b, 0, c))
    full = lambda a: pl.BlockSpec(a.shape, lambda b: (0, 0))
    out = pl.BlockSpec((1, S, 512), lambda b: (b, 0, 0))
    return pl.pallas_call(
        _ret_conv_body,
        grid=(B,),
        in_specs=[col(c) for c in range(7)] + [full(gn), full(conv_w)],
        out_specs=[out, out],
        out_shape=[jax.ShapeDtypeStruct((B, S, 512), BF16)] * 2,
        scratch_shapes=[pltpu.VMEM((S + 8, D_CH), F32)],
        compiler_params=_cparams(("parallel",)),
        name="retention_conv",
    )(z, z, z, z, z, z, z, gn, conv_w)


def _out_proj_route_body(x_ref, a_ref, b_ref, wa_ref, wb_ref, g_ref, wr_ref, o_ref, route_ref):
    x = x_ref[...] + _dot(a_ref[...], wa_ref[...]) + _dot(b_ref[...], wb_ref[...])
    o_ref[...] = x
    h = _rms(x, g_ref[...])
    h_hi = h.astype(BF16)
    h_lo = (h - h_hi.astype(F32)).astype(BF16)
    hi_pass = _dot(h_hi, wr_ref[...])
    logits = hi_pass[:, :LANE] + hi_pass[:, LANE:] + _dot(h_lo, wr_ref[:, :LANE])
    lane = lax.broadcasted_iota(jnp.int32, logits.shape, 1)
    logits = jnp.where(lane < N_EXPERTS, logits, NEG_INF)
    m1 = jnp.max(logits, axis=-1, keepdims=True)
    i1 = jnp.min(jnp.where(logits == m1, lane, LANE), axis=-1, keepdims=True)
    rest = jnp.where(lane == i1, NEG_INF, logits)
    m2 = jnp.max(rest, axis=-1, keepdims=True)
    i2 = jnp.min(jnp.where(rest == m2, lane, LANE), axis=-1, keepdims=True)
    e2 = jnp.exp(m2 - m1)
    g1 = 1.0 / (1.0 + e2)
    g2 = e2 / (1.0 + e2)
    route_ref[...] = jnp.where(lane == 0, i1.astype(F32),
                               jnp.where(lane == 1, i2.astype(F32),
                                         jnp.where(lane == 2, g1, jnp.where(lane == 3, g2, 0.0))))


def _out_proj_route(x2, a, b, wa, wb, g, wr):
    T = x2.shape[0]
    tm = ROW_TILE
    row = lambda n: pl.BlockSpec((tm, n), lambda i: (i, 0))
    full = lambda w: pl.BlockSpec(w.shape, lambda i: (0, 0))
    return pl.pallas_call(
        _out_proj_route_body,
        grid=(T // tm,),
        in_specs=[row(D_MODEL), row(a.shape[1]), row(b.shape[1]), full(wa), full(wb), full(g), full(wr)],
        out_specs=[row(D_MODEL), row(LANE)],
        out_shape=[jax.ShapeDtypeStruct((T, D_MODEL), F32), jax.ShapeDtypeStruct((T, LANE), F32)],
        compiler_params=_cparams(("parallel",)),
        name="out_proj_router",
    )(x2, a, b, wa, wb, g, wr)


def _row_copy(src, dst, sem, src_row, dst_row):
    return pltpu.make_async_copy(src.at[pl.ds(src_row, 1)], dst.at[pl.ds(dst_row, 1)], sem)


def _dispatch_body(pos_ref, fill_ref, nu_ref, x_hbm, o_hbm, xbuf, zero_scr, load_sem, scat_sem,
                   zero_sem):
    i = pl.program_id(0)
    nb = pl.num_programs(0)
    nt = xbuf.shape[1]
    tm = ROW_TILE

    def zero_tile(row0):
        fill = pltpu.make_async_copy(zero_scr, o_hbm.at[pl.ds(pl.multiple_of(row0, tm), tm)], zero_sem)
        fill.start()
        fill.wait()

    @pl.when(i == 0)
    def _():
        zero_scr[...] = jnp.zeros_like(zero_scr)
        for e in range(N_EXPERTS):
            @pl.when(fill_ref[e] >= 0)
            def _():
                zero_tile(fill_ref[e])

        def tail(t, carry):
            zero_tile(t * tm)
            return carry

        lax.fori_loop(nu_ref[0], o_hbm.shape[0] // tm, tail, 0)

    def load(block, slot):
        return pltpu.make_async_copy(x_hbm.at[pl.ds(block * nt, nt)], xbuf.at[slot], load_sem.at[slot])

    def wait_scatter(slot):
        for _ in range(TOP_K):
            pltpu.make_async_copy(xbuf.at[slot], o_hbm.at[pl.ds(0, nt)], scat_sem.at[slot]).wait()

    slot = i % 3
    nxt = (i + 1) % 3

    @pl.when(i == 0)
    def _():
        load(0, 0).start()

    @pl.when(i >= 2)
    def _():
        wait_scatter(nxt)

    @pl.when(i + 1 < nb)
    def _():
        load(i + 1, nxt).start()

    load(i, slot).wait()
    base = i * nt * TOP_K

    def issue(r, carry):
        for k in range(TOP_K):
            _row_copy(xbuf.at[slot], o_hbm, scat_sem.at[slot], r, pos_ref[base + r * TOP_K + k]).start()
        return carry

    lax.fori_loop(0, nt, issue, 0, unroll=8)

    @pl.when(i == nb - 1)
    def _():
        @pl.when(i >= 1)
        def _():
            wait_scatter((i + 2) % 3)
        wait_scatter(slot)


def _dispatch(pos, fill_rows, n_used, x2, n_rows_pad):
    nt = GATHER_ROWS
    grid_spec = pltpu.PrefetchScalarGridSpec(
        num_scalar_prefetch=3,
        grid=(x2.shape[0] // nt,),
        in_specs=[pl.BlockSpec(memory_space=pl.ANY)],
        out_specs=pl.BlockSpec(memory_space=pl.ANY),
        scratch_shapes=[pltpu.VMEM((3, nt, D_MODEL), F32), pltpu.VMEM((ROW_TILE, D_MODEL), F32),
                        pltpu.SemaphoreType.DMA((3,)), pltpu.SemaphoreType.DMA((3,)),
                        pltpu.SemaphoreType.DMA(())],
    )
    return pl.pallas_call(
        _dispatch_body,
        grid_spec=grid_spec,
        out_shape=jax.ShapeDtypeStruct((n_rows_pad, D_MODEL), F32),
        compiler_params=_cparams(("arbitrary",)),
        name="dispatch_rows",
    )(pos, fill_rows, n_used, x2)


def _combine_body(pos_ref, x_ref, route_ref, g_ref, ys_hbm, o_ref, buf, sem):
    i = pl.program_id(0)
    n = x_ref.shape[0]

    def fetch(block, slot):
        base = block * n

        def issue(r, carry):
            for k in range(TOP_K):
                _row_copy(ys_hbm, buf.at[slot, k], sem.at[slot],
                          pos_ref[(base + r) * TOP_K + k], r).start()
            return carry

        lax.fori_loop(0, n, issue, 0, unroll=8)

    slot = i % 2

    @pl.when(i == 0)
    def _():
        fetch(0, 0)

    @pl.when(i + 1 < pl.num_programs(0))
    def _():
        fetch(i + 1, 1 - slot)

    for k in range(TOP_K):
        pltpu.make_async_copy(ys_hbm.at[pl.ds(0, n)], buf.at[slot, k], sem.at[slot]).wait()
    route = route_ref[...]
    y = x_ref[...] + route[:, 2:3] * buf[slot, 0] + route[:, 3:4] * buf[slot, 1]
    o_ref[...] = _rms(y, g_ref[...])


def _combine(pos, x2, route, g, ys):
    T = x2.shape[0]
    n = GATHER_ROWS
    grid_spec = pltpu.PrefetchScalarGridSpec(
        num_scalar_prefetch=1,
        grid=(T // n,),
        in_specs=[pl.BlockSpec((n, D_MODEL), lambda i, p: (i, 0)),
                  pl.BlockSpec((n, LANE), lambda i, p: (i, 0)),
                  pl.BlockSpec((1, D_MODEL), lambda i, p: (0, 0)),
                  pl.BlockSpec(memory_space=pl.ANY)],
        out_specs=pl.BlockSpec((n, D_MODEL), lambda i, p: (i, 0)),
        scratch_shapes=[pltpu.VMEM((2, TOP_K, n, D_MODEL), F32), pltpu.SemaphoreType.DMA((2,))],
    )
    return pl.pallas_call(
        _combine_body,
        grid_spec=grid_spec,
        out_shape=jax.ShapeDtypeStruct((T, D_MODEL), F32),
        compiler_params=_cparams(("arbitrary",)),
        name="combine_norm",
    )(pos, x2, route, g, ys)


def _even_weights(w_in, w_uq, w_ukv):
    n_main = w_in.shape[1] - B_ROPE
    w_kr = jnp.pad(w_in[:, n_main:], ((0, 0), (B_NOPE, LANE - B_NOPE - B_ROPE)))
    w_in_cat = jnp.concatenate([w_in[:, :n_main], w_kr], axis=1).astype(BF16)
    uq = w_uq.reshape(B_Q_LORA, B_HEADS, B_NOPE + B_ROPE)
    uq = jnp.pad(uq, ((0, 0), (0, 0), (0, LANE - B_NOPE - B_ROPE)))
    wuq = uq.reshape(B_Q_LORA, B_HEADS * LANE).astype(BF16)
    ukv = w_ukv.reshape(B_KV_LORA, B_HEADS, B_NOPE + B_V)
    wuk = ukv[..., :B_NOPE].reshape(B_KV_LORA, B_HEADS * B_NOPE).astype(BF16)
    wuv = ukv[..., B_NOPE:].reshape(B_KV_LORA, B_HEADS * B_V).astype(BF16)
    return w_in_cat, wuq, wuk, wuv


def _routing(route, n_rows_pad):
    tm = ROW_TILE
    e_flat = route[:, :TOP_K].astype(jnp.int32).reshape(-1)
    onehot = (e_flat[:, None] == jnp.arange(N_EXPERTS, dtype=jnp.int32)[None, :]).astype(jnp.int32)
    csum = jnp.cumsum(onehot, axis=0)
    rank = jnp.sum(csum * onehot, axis=1) - 1
    counts = csum[-1]
    tiles_e = (counts + tm - 1) // tm
    tiles_end = jnp.cumsum(tiles_e)
    row_off = (tiles_end - tiles_e) * tm
    pos = (jnp.sum(onehot * row_off[None, :], axis=1) + rank).astype(jnp.int32)
    n_tiles = n_rows_pad // tm
    tile_expert = jnp.sum(jnp.arange(n_tiles, dtype=jnp.int32)[:, None] >= tiles_end[None, :], axis=1)
    tile_expert = jnp.minimum(tile_expert, N_EXPERTS - 1).astype(jnp.int32)
    n_used = tiles_end[-1:].astype(jnp.int32)
    fill_rows = jnp.where(tiles_e > 0, (tiles_end - 1) * tm, -1).astype(jnp.int32)
    return pos, tile_expert, n_used, fill_rows


def kernel(x, positions, even_norm_mix, even_w_in, even_sinks, even_q_norm, even_w_uq, even_kv_norm, even_w_ukv, even_w_out, even_norm_ffn, even_w_gate, even_w_up, even_w_down, odd_norm_mix, odd_w_in, odd_ret_gn, odd_conv_w, odd_w_out, odd_norm_ffn, odd_router, odd_we_gate, odd_we_up, odd_we_down, final_norm):
    B, S, D = x.shape
    T = B * S
    tm = ROW_TILE
    x2 = x.reshape(T, D)
    pos2 = positions.reshape(T, 1)

    inv_freq = ROPE_THETA ** (-np.arange(0, B_ROPE, 2, dtype=np.float32) / B_ROPE)
    inv_row = np.zeros((1, LANE), np.float32)
    inv_row[0, B_NOPE:B_NOPE + B_ROPE // 2] = inv_freq
    inv_row[0, B_NOPE + B_ROPE // 2:B_NOPE + B_ROPE] = inv_freq
    slopes = 2.0 ** (-8.0 * (np.arange(A_HEADS, dtype=np.float32) + 1.0) / A_HEADS)
    slopes = jnp.asarray(slopes * LOG2E, F32)

    w_in_cat, wuq, wuk, wuv = _even_weights(even_w_in[0], even_w_uq[0], even_w_ukv[0])
    (qa, ka, valo, vahi, qb, kb, vb), (w_gate, w_up, w_down) = _even_in(
        x2, pos2, even_norm_mix[0][None], w_in_cat, jnp.asarray(inv_row),
        even_q_norm[0][None], wuq, even_kv_norm[0][None], wuk, wuv,
        [even_w_gate[0], even_w_up[0], even_w_down[0]])
    r3 = lambda t: t.reshape(B, S, t.shape[-1])
    ya = _swa(slopes, even_sinks[0] * LOG2E, r3(qa), r3(ka), r3(valo), r3(vahi)).reshape(T, -1)
    expert_w = (odd_we_gate[0], odd_we_up[0], odd_we_down[0])
    yb, expert_w_bf16 = _mla(r3(qb), r3(kb), r3(vb), [w.reshape(-1, w.shape[-1]) for w in expert_w])
    yb = yb.reshape(T, -1)
    we_gate, we_up, we_down = [wb.reshape(w.shape) for wb, w in zip(expert_w_bf16, expert_w)]
    w_out = even_w_out[0].astype(BF16)
    x2 = _out_proj_dense_ffn(x2, ya, yb, w_out[:ya.shape[1]], w_out[ya.shape[1]:],
                             even_norm_ffn[0][None], w_gate, w_up, w_down)

    z = _odd_in(x2, odd_norm_mix[0][None], odd_w_in[0].astype(BF16))
    yc, yd = _ret_conv(r3(z), odd_ret_gn[0][None], odd_conv_w[0])
    w_out = odd_w_out[0].astype(BF16)
    wr = jnp.pad(odd_router[0], ((0, 0), (0, LANE - N_EXPERTS)))
    wr_hi = wr.astype(BF16)
    wr_lo = (wr - wr_hi.astype(F32)).astype(BF16)
    x2, route = _out_proj_route(x2, yc.reshape(T, -1), yd.reshape(T, -1), w_out[:C_HEADS * C_VAL_DIM],
                                w_out[C_HEADS * C_VAL_DIM:], odd_norm_ffn[0][None],
                                jnp.concatenate([wr_hi, wr_lo], axis=1))
    n_rows_pad = T * TOP_K + N_EXPERTS * tm
    pos, tile_expert, n_used, fill_rows = _routing(route, n_rows_pad)
    xs = _dispatch(pos, fill_rows, n_used, x2, n_rows_pad)
    ys = _moe_ffn(tile_expert, n_used, xs, odd_norm_ffn[0][None], we_gate, we_up, we_down)
    out = _combine(pos, x2, route, final_norm[None], ys)
    return out.reshape(B, S, D)
```

```python
import math

import numpy as np
import jax
import jax.numpy as jnp
from jax import lax
from jax.experimental import pallas as pl
from jax.experimental.pallas import tpu as pltpu

F32 = jnp.float32
BF16 = jnp.bfloat16

D_MODEL = 1024
CHUNK = 64
A_HEADS = 8
A_KV_HEADS = 2
A_HEAD_DIM = 64
A_GROUP = A_HEADS // A_KV_HEADS
WINDOW_CHUNKS = 2
B_HEADS = 8
B_Q_LORA = 384
B_KV_LORA = 256
B_NOPE = 64
B_ROPE = 32
B_V = 64
ROPE_THETA = 10000.0
C_HEADS = 4
C_KEY_DIM = 128
C_VAL_DIM = 128
D_CH = 512
CONV_W = 3
D_FF = 3584
N_EXPERTS = 8
TOP_K = 2
EPS = 1e-6
NEG_INF = -1e30
LOG2E = math.log2(math.e)

LANE = 128
ROW_TILE = 512
FF_TILE = 1792
ATT_Q = 128
MLA_T = 256
RET_BLOCK = 256
GATHER_ROWS = 256
TOKEN_ROWS = D_MODEL // LANE
VMEM_LIMIT = 56 * 1024 * 1024

_EV_OFF = np.cumsum([0, A_HEADS * A_HEAD_DIM, A_KV_HEADS * A_HEAD_DIM, A_KV_HEADS * A_HEAD_DIM,
                     B_Q_LORA, B_KV_LORA, LANE])


def _cparams(semantics):
    return pltpu.CompilerParams(dimension_semantics=semantics, vmem_limit_bytes=VMEM_LIMIT)


def _rms(xf, g):
    return xf * lax.rsqrt(jnp.mean(xf * xf, axis=-1, keepdims=True) + EPS) * g


def _dot(a, b):
    return jnp.dot(a, b, preferred_element_type=F32)


def _dot_nt(a, b):
    return lax.dot_general(a, b, (((1,), (1,)), ((), ())), preferred_element_type=F32)


def _even_in_body(x_ref, pos_ref, g_ref, win_ref, invf_ref, qn_ref, wuq_ref, kvn_ref, wuk_ref, wuv_ref,
                  *rest):
    n_cast = (len(rest) - 7) // 2
    qa_ref, ka_ref, valo_ref, vahi_ref, qb_ref, kb_ref, vb_ref = rest[n_cast:n_cast + 7]
    for src, dst in zip(rest[:n_cast], rest[n_cast + 7:]):
        dst[...] = src[...].astype(BF16)
    tm = x_ref.shape[0]
    h = _rms(x_ref[...], g_ref[...]).astype(BF16)
    lane = lax.broadcasted_iota(jnp.int32, (tm, LANE), 1)
    lo = lane < LANE // 2

    def proj(k):
        return _dot(h, win_ref[:, int(_EV_OFF[k]):int(_EV_OFF[k + 1])])

    def swap_halves(t):
        return pltpu.roll(t, LANE // 2, axis=1)

    def split_pair(t, out_ref, first_tile):
        out_ref[:, first_tile * LANE:(first_tile + 1) * LANE] = jnp.where(lo, t, 0.0).astype(BF16)
        out_ref[:, (first_tile + 1) * LANE:(first_tile + 2) * LANE] = (
            jnp.where(lo, swap_halves(t), 0.0).astype(BF16))

    qa = proj(0) * (A_HEAD_DIM ** -0.5 * LOG2E)
    for pair in range(A_HEADS // 2):
        split_pair(qa[:, pair * LANE:(pair + 1) * LANE], qa_ref, 2 * pair)
    split_pair(proj(1), ka_ref, 0)
    va = proj(2)
    va_swapped = swap_halves(va)
    valo_ref[:, :LANE] = jnp.where(lo, va, 0.0).astype(BF16)
    valo_ref[:, LANE:] = jnp.where(lo, va_swapped, 0.0).astype(BF16)
    vahi_ref[:, :LANE] = jnp.where(lo, 0.0, va_swapped).astype(BF16)
    vahi_ref[:, LANE:] = jnp.where(lo, 0.0, va).astype(BF16)

    ang = pos_ref[...].astype(F32) * invf_ref[...]
    is_rope = (lane >= B_NOPE) & (lane < B_NOPE + B_ROPE)
    cosm = jnp.where(lane < B_NOPE, 1.0, jnp.where(is_rope, jnp.cos(ang), 0.0))
    sinm = jnp.where(is_rope, jnp.sin(ang), 0.0)
    first_half = lane < B_NOPE + B_ROPE // 2

    def rope(t):
        rot = jnp.where(first_half, -pltpu.roll(t, LANE - B_ROPE // 2, axis=1),
                        pltpu.roll(t, B_ROPE // 2, axis=1))
        return t * cosm + rot * sinm

    cq = _rms(proj(3), qn_ref[...]).astype(BF16)
    q_all = _dot(cq, wuq_ref[...])
    scale = (B_NOPE + B_ROPE) ** -0.5 * LOG2E
    for hd in range(B_HEADS):
        sl = slice(hd * LANE, (hd + 1) * LANE)
        qb_ref[:, sl] = (rope(q_all[:, sl]) * scale).astype(BF16)

    k_rope = rope(proj(5))
    ckv = _rms(proj(4), kvn_ref[...]).astype(BF16)
    k_all = _dot(ckv, wuk_ref[...])
    v_all = _dot(ckv, wuv_ref[...])
    ones_lo = jnp.where(lane == LANE // 2, 1.0, 0.0)
    ones_hi = jnp.where(lane == 0, 1.0, 0.0)
    for pair in range(B_HEADS // 2):
        sl = slice(pair * LANE, (pair + 1) * LANE)
        even = slice(2 * pair * LANE, (2 * pair + 1) * LANE)
        odd = slice((2 * pair + 1) * LANE, (2 * pair + 2) * LANE)
        kb_ref[:, even] = (jnp.where(lo, k_all[:, sl], 0.0) + k_rope).astype(BF16)
        kb_ref[:, odd] = (jnp.where(lo, swap_halves(k_all[:, sl]), 0.0) + k_rope).astype(BF16)
        vb_ref[:, even] = jnp.where(lo, v_all[:, sl], ones_lo).astype(BF16)
        vb_ref[:, odd] = jnp.where(lo, ones_hi, v_all[:, sl]).astype(BF16)


def _even_in(x2, pos2, g, w_in, inv_freq_row, q_norm, wuq, kv_norm, wuk, wuv, f32_weights):
    T = x2.shape[0]
    tm = ROW_TILE
    steps = T // tm
    row = lambda n: pl.BlockSpec((tm, n), lambda i: (i, 0))
    full = lambda a: pl.BlockSpec(a.shape, lambda i: (0, 0))
    slab = lambda w: pl.BlockSpec((w.shape[0] // steps, w.shape[1]), lambda i: (i, 0))
    outs = [A_HEADS * LANE, A_KV_HEADS * LANE, A_KV_HEADS * LANE, A_KV_HEADS * LANE,
            B_HEADS * LANE, B_HEADS * LANE, B_HEADS * LANE]
    res = pl.pallas_call(
        _even_in_body,
        grid=(steps,),
        in_specs=[row(D_MODEL), row(1), full(g), full(w_in), full(inv_freq_row), full(q_norm),
                  full(wuq), full(kv_norm), full(wuk), full(wuv)] + [slab(w) for w in f32_weights],
        out_specs=[row(n) for n in outs] + [slab(w) for w in f32_weights],
        out_shape=[jax.ShapeDtypeStruct((T, n), BF16) for n in outs]
        + [jax.ShapeDtypeStruct(w.shape, BF16) for w in f32_weights],
        compiler_params=_cparams(("parallel",)),
        name="even_in_proj",
    )(x2, pos2, g, w_in, inv_freq_row, q_norm, wuq, kv_norm, wuk, wuv, *f32_weights)
    return res[:7], res[7:]


def _swa_body(slopes_ref, sinks_ref, q_ref, k_ref, vlo_ref, vhi_ref, o_ref):
    hk = pl.program_id(1)
    S = q_ref.shape[1]
    win = ATT_Q + WINDOW_CHUNKS * CHUNK
    slope = [slopes_ref[hk * A_GROUP + g] for g in range(A_GROUP)]
    sink = [sinks_ref[hk * A_GROUP + g] for g in range(A_GROUP)]

    def mask_bias(delta):
        qpos = delta + lax.broadcasted_iota(jnp.int32, (ATT_Q, win), 0)
        kpos = lax.broadcasted_iota(jnp.int32, (ATT_Q, win), 1)
        qc = qpos // CHUNK
        kc = kpos // CHUNK
        valid = (kc <= qc) & (kc >= qc - WINDOW_CHUNKS)
        dist = jnp.abs(qpos - kpos).astype(F32)
        return [jnp.where(valid, -slope[g] * dist, NEG_INF) for g in range(A_GROUP)]

    bias_by_delta = {0: mask_bias(0), WINDOW_CHUNKS * CHUNK: mask_bias(WINDOW_CHUNKS * CHUNK)}
    for i in range(S // ATT_Q):
        q0 = i * ATT_Q
        k0 = max(q0 - WINDOW_CHUNKS * CHUNK, 0)
        bias = bias_by_delta[q0 - k0]
        kw = k_ref[0, k0:k0 + win, :]
        vw = (vlo_ref[0, k0:k0 + win, :], vhi_ref[0, k0:k0 + win, :])
        for pair in range(A_GROUP // 2):
            o = jnp.zeros((ATT_Q, LANE), F32)
            for w in range(2):
                g = pair * 2 + w
                s = _dot_nt(q_ref[0, q0:q0 + ATT_Q, g * LANE:(g + 1) * LANE], kw) + bias[g]
                m = jnp.maximum(jnp.max(s, axis=-1, keepdims=True), sink[g])
                p = jnp.exp2(s - m)
                den = jnp.sum(p, axis=-1, keepdims=True) + jnp.exp2(sink[g] - m)
                o = o + _dot(p.astype(BF16), vw[w]) * (1.0 / den)
            o_ref[0, q0:q0 + ATT_Q, pair * LANE:(pair + 1) * LANE] = o.astype(BF16)


def _swa(slopes, sinks, qa, ka, valo, vahi):
    B, S, _ = qa.shape
    smem = pl.BlockSpec(memory_space=pltpu.SMEM)
    kv = pl.BlockSpec((1, S, LANE), lambda b, h: (b, 0, h))
    return pl.pallas_call(
        _swa_body,
        grid=(B, A_KV_HEADS),
        in_specs=[smem, smem, pl.BlockSpec((1, S, A_GROUP * LANE), lambda b, h: (b, 0, h)), kv, kv, kv],
        out_specs=pl.BlockSpec((1, S, A_GROUP * A_HEAD_DIM), lambda b, h: (b, 0, h)),
        out_shape=jax.ShapeDtypeStruct((B, S, A_HEADS * A_HEAD_DIM), BF16),
        compiler_params=_cparams(("parallel", "parallel")),
        name="swa_attention",
    )(slopes, sinks, qa, ka, valo, vahi)


def _mla_body(q_ref, k_ref, v_ref, *rest):
    n_cast = (len(rest) - 1) // 2
    o_ref = rest[n_cast]
    for src, dst in zip(rest[:n_cast], rest[n_cast + 1:]):
        dst[...] = src[...].astype(BF16)
    S = q_ref.shape[1]
    T = MLA_T
    row = lax.broadcasted_iota(jnp.int32, (T, T), 0) // CHUNK
    col = lax.broadcasted_iota(jnp.int32, (T, T), 1) // CHUNK
    diag_ok = col <= row
    lo = lax.broadcasted_iota(jnp.int32, (T, LANE), 1) < LANE // 2

    for i in range(S // T):
        q0 = i * T
        normed = []
        for hh in range(2):
            sl = slice(hh * LANE, (hh + 1) * LANE)
            q = q_ref[0, q0:q0 + T, sl]
            s_diag = jnp.where(diag_ok, _dot_nt(q, k_ref[0, q0:q0 + T, sl]), NEG_INF)
            m = jnp.max(s_diag, axis=-1, keepdims=True)
            if i > 0:
                s_past = _dot_nt(q, k_ref[0, 0:q0, sl])
                m = jnp.maximum(m, jnp.max(s_past, axis=-1, keepdims=True))
            acc = _dot(jnp.exp2(s_diag - m).astype(BF16), v_ref[0, q0:q0 + T, sl])
            if i > 0:
                acc = acc + _dot(jnp.exp2(s_past - m).astype(BF16), v_ref[0, 0:q0, sl])
            den_lane = LANE // 2 if hh == 0 else 0
            normed.append(acc * (1.0 / acc[:, den_lane:den_lane + 1]))
        o_ref[0, q0:q0 + T, :] = jnp.where(lo, normed[0], normed[1]).astype(BF16)


def _mla(qb, kb, vb, f32_weights):
    B, S, _ = qb.shape
    n_pairs = B_HEADS // 2
    steps = B * n_pairs
    spec = pl.BlockSpec((1, S, 2 * LANE), lambda b, p: (b, 0, p))
    slab = lambda w: pl.BlockSpec((w.shape[0] // steps, w.shape[1]), lambda b, p: (b * n_pairs + p, 0))
    outs = pl.pallas_call(
        _mla_body,
        grid=(B, n_pairs),
        in_specs=[spec, spec, spec] + [slab(w) for w in f32_weights],
        out_specs=[pl.BlockSpec((1, S, 2 * B_V), lambda b, p: (b, 0, p))] + [slab(w) for w in f32_weights],
        out_shape=[jax.ShapeDtypeStruct((B, S, B_HEADS * B_V), BF16)]
        + [jax.ShapeDtypeStruct(w.shape, BF16) for w in f32_weights],
        compiler_params=_cparams(("parallel", "parallel")),
        name="mla_attention",
    )(qb, kb, vb, *f32_weights)
    return outs[0], outs[1:]


def _swiglu_partial(h, wg, wu, wd):
    gate = _dot(h, wg)
    up = _dot(h, wu)
    return _dot((gate * jax.nn.sigmoid(gate) * up).astype(BF16), wd)


def _dense_ffn_body(x_ref, a_ref, b_ref, wa_ref, wb_ref, g_ref, wg_ref, wu_ref, wd_ref, o_ref, h_scr):
    @pl.when(pl.program_id(1) == 0)
    def _():
        x1 = x_ref[...] + _dot(a_ref[...], wa_ref[...]) + _dot(b_ref[...], wb_ref[...])
        h_scr[...] = _rms(x1, g_ref[...]).astype(BF16)
        o_ref[...] = x1

    o_ref[...] += _swiglu_partial(h_scr[...], wg_ref[...], wu_ref[...], wd_ref[...])


def _out_proj_dense_ffn(x2, a, b, wa, wb, g, wg, wu, wd):
    T = x2.shape[0]
    tm, tf = ROW_TILE, FF_TILE
    row = lambda n: pl.BlockSpec((tm, n), lambda i, j: (i, 0))
    full = lambda w: pl.BlockSpec(w.shape, lambda i, j: (0, 0))
    return pl.pallas_call(
        _dense_ffn_body,
        grid=(T // tm, D_FF // tf),
        in_specs=[row(D_MODEL), row(a.shape[1]), row(b.shape[1]), full(wa), full(wb), full(g),
                  pl.BlockSpec((D_MODEL, tf), lambda i, j: (0, j)),
                  pl.BlockSpec((D_MODEL, tf), lambda i, j: (0, j)),
                  pl.BlockSpec((tf, D_MODEL), lambda i, j: (j, 0))],
        out_specs=row(D_MODEL),
        out_shape=jax.ShapeDtypeStruct((T, D_MODEL), F32),
        scratch_shapes=[pltpu.VMEM((tm, D_MODEL), BF16)],
        compiler_params=_cparams(("parallel", "arbitrary")),
        name="out_proj_swiglu",
    )(x2, a, b, wa, wb, g, wg, wu, wd)


def _token_chunk(ref, c, n):
    return ref.at[pl.ds(c, n, stride=TOKEN_ROWS), :]


def _moe_ffn_body(te_ref, nu_ref, x_ref, g_ref, wg_ref, wu_ref, wd_ref, o_ref, h_scr, acc_scr):
    i = pl.program_id(0)
    j = pl.program_id(1)
    nj = pl.num_programs(1)
    tm = h_scr.shape[0]

    @pl.when(i < nu_ref[0])
    def _():
        @pl.when(j == 0)
        def _():
            x = jnp.concatenate([_token_chunk(x_ref, c, tm)[...] for c in range(TOKEN_ROWS)], axis=1)
            h_scr[...] = _rms(x, g_ref[...]).astype(BF16)

        part = _swiglu_partial(h_scr[...], wg_ref[0], wu_ref[0], wd_ref[0])

        @pl.when(j == 0)
        def _():
            acc_scr[...] = part

        @pl.when((j > 0) & (j < nj - 1))
        def _():
            acc_scr[...] += part

        @pl.when(j == nj - 1)
        def _():
            res = acc_scr[...] + part
            for c in range(TOKEN_ROWS):
                _token_chunk(o_ref, c, tm)[...] = res[:, c * LANE:(c + 1) * LANE]

    @pl.when((i >= nu_ref[0]) & (j == 0))
    def _():
        o_ref[...] = jnp.zeros_like(o_ref)


def _moe_ffn(tile_expert, n_used, xs, g, wg, wu, wd):
    P = xs.shape[0] // TOKEN_ROWS
    tm, tf = ROW_TILE, FF_TILE
    nj = D_FF // tf
    assert nj >= 2

    def tile(i, nu):
        return jnp.minimum(i, nu[0] - 1)

    def ffcol(i, j, nu):
        return jnp.where(i < nu[0], j, nj - 1)

    grid_spec = pltpu.PrefetchScalarGridSpec(
        num_scalar_prefetch=2,
        grid=(P // tm, nj),
        in_specs=[
            pl.BlockSpec((tm * TOKEN_ROWS, LANE), lambda i, j, te, nu: (tile(i, nu), 0)),
            pl.BlockSpec((1, D_MODEL), lambda i, j, te, nu: (0, 0)),
            pl.BlockSpec((1, D_MODEL, tf), lambda i, j, te, nu: (te[tile(i, nu)], 0, ffcol(i, j, nu))),
            pl.BlockSpec((1, D_MODEL, tf), lambda i, j, te, nu: (te[tile(i, nu)], 0, ffcol(i, j, nu))),
            pl.BlockSpec((1, tf, D_MODEL), lambda i, j, te, nu: (te[tile(i, nu)], ffcol(i, j, nu), 0)),
        ],
        out_specs=pl.BlockSpec((tm * TOKEN_ROWS, LANE), lambda i, j, te, nu: (i, 0)),
        scratch_shapes=[pltpu.VMEM((tm, D_MODEL), BF16), pltpu.VMEM((tm, D_MODEL), F32)],
    )
    return pl.pallas_call(
        _moe_ffn_body,
        grid_spec=grid_spec,
        out_shape=jax.ShapeDtypeStruct((P * TOKEN_ROWS, LANE), F32),
        compiler_params=_cparams(("arbitrary", "arbitrary")),
        name="swiglu_experts",
    )(tile_expert, n_used, xs, g, wg, wu, wd)


def _odd_in_body(x_ref, g_ref, w_ref, o_ref):
    h = _rms(x_ref[...], g_ref[...]).astype(BF16)
    n = o_ref.shape[1]
    step = 512
    for c in range(n // step):
        sl = slice(c * step, (c + 1) * step)
        o_ref[:, sl] = _dot(h, w_ref[:, sl]).astype(BF16)


def _odd_in(x2, g, w):
    T = x2.shape[0]
    tm = ROW_TILE
    n = w.shape[1]
    return pl.pallas_call(
        _odd_in_body,
        grid=(T // tm,),
        in_specs=[pl.BlockSpec((tm, D_MODEL), lambda i: (i, 0)),
                  pl.BlockSpec(g.shape, lambda i: (0, 0)),
                  pl.BlockSpec(w.shape, lambda i: (0, 0))],
        out_specs=pl.BlockSpec((tm, n), lambda i: (i, 0)),
        out_shape=jax.ShapeDtypeStruct((T, n), BF16),
        compiler_params=_cparams(("parallel",)),
        name="odd_in_proj",
    )(x2, g, w)


def _ret_conv_body(q_ref, k_ref, v_ref, gt_ref, bd_ref, cd_ref, hd_ref, gn_ref, cw_ref,
                   yc_ref, yd_ref, u_scr):
    S = q_ref.shape[1]
    C = RET_BLOCK
    dk = C_KEY_DIM
    scale = dk ** -0.5
    ri = lax.broadcasted_iota(jnp.int32, (C, C), 0)
    ci = lax.broadcasted_iota(jnp.int32, (C, C), 1)
    diff = (ri - ci).astype(F32)
    pos = lax.broadcasted_iota(jnp.int32, (C, 1), 0).astype(F32)
    for hd in range(C_HEADS):
        log_g = math.log(1.0 - 2.0 ** (-5.0 - hd))
        intra = jnp.where(diff >= 0, jnp.exp(log_g * jnp.maximum(diff, 0.0)), 0.0) * scale
        cross = jnp.exp(log_g * (pos + 1.0))
        state = jnp.exp(log_g * (C - 1.0 - pos)) * scale
        block_decay = math.exp(log_g * C)
        sl = slice(hd * dk, (hd + 1) * dk)
        R = None
        for n in range(S // C):
            rows = slice(n * C, (n + 1) * C)
            q = q_ref[0, rows, sl]
            k = k_ref[0, rows, sl]
            v = v_ref[0, rows, sl]
            scores = _dot_nt(q, k) * intra
            y = _dot(scores.astype(BF16), v)
            kd_t = (k.astype(F32) * state).T.astype(BF16)
            kv = _dot(kd_t, v)
            if R is None:
                R = kv
            else:
                y = y + _dot(q, R.astype(BF16)) * cross
                R = block_decay * R + kv
            mu = jnp.mean(y, axis=-1, keepdims=True)
            yc = y - mu
            var = jnp.mean(yc * yc, axis=-1, keepdims=True)
            yn = yc * lax.rsqrt(var + EPS) * gn_ref[:, sl]
            gate = gt_ref[0, rows, sl].astype(F32)
            yc_ref[0, rows, sl] = (gate * jax.nn.sigmoid(gate) * yn).astype(BF16)

    halo = 8
    u_scr[0:halo, :] = jnp.zeros((halo, D_CH), F32)
    blk = 256
    for r in range(S // blk):
        rows = slice(r * blk, (r + 1) * blk)
        u_scr[halo + r * blk:halo + (r + 1) * blk, :] = (
            cd_ref[0, rows, :].astype(F32) * hd_ref[0, rows, :].astype(F32))
    for r in range(S // blk):
        acc = jnp.zeros((blk, D_CH), F32)
        for jw in range(CONV_W):
            lo = halo - (CONV_W - 1) + jw + r * blk
            acc = acc + cw_ref[jw:jw + 1, :] * u_scr[lo:lo + blk, :]
        rows = slice(r * blk, (r + 1) * blk)
        yd_ref[0, rows, :] = (bd_ref[0, rows, :].astype(F32) * acc).astype(BF16)


def _ret_conv(z, gn, conv_w):
    B, S, _ = z.shape
    col = lambda c: pl.BlockSpec((1, S, 512), lambda b: (b, 0, c))
    full = lambda a: pl.BlockSpec(a.shape, lambda b: (0, 0))
    out = pl.BlockSpec((1, S, 512), lambda b: (b, 0, 0))
    return pl.pallas_call(
        _ret_conv_body,
        grid=(B,),
        in_specs=[col(c) for c in range(7)] + [full(gn), full(conv_w)],
        out_specs=[out, out],
        out_shape=[jax.ShapeDtypeStruct((B, S, 512), BF16)] * 2,
        scratch_shapes=[pltpu.VMEM((S + 8, D_CH), F32)],
        compiler_params=_cparams(("parallel",)),
        name="retention_conv",
    )(z, z, z, z, z, z, z, gn, conv_w)


def _out_proj_route_body(x_ref, a_ref, b_ref, wa_ref, wb_ref, g_ref, wr_ref, o_ref, route_ref):
    x = x_ref[...] + _dot(a_ref[...], wa_ref[...]) + _dot(b_ref[...], wb_ref[...])
    o_ref[...] = x
    h = _rms(x, g_ref[...])
    h_hi = h.astype(BF16)
    h_lo = (h - h_hi.astype(F32)).astype(BF16)
    hi_pass = _dot(h_hi, wr_ref[...])
    logits = hi_pass[:, :LANE] + hi_pass[:, LANE:] + _dot(h_lo, wr_ref[:, :LANE])
    lane = lax.broadcasted_iota(jnp.int32, logits.shape, 1)
    logits = jnp.where(lane < N_EXPERTS, logits, NEG_INF)
    m1 = jnp.max(logits, axis=-1, keepdims=True)
    i1 = jnp.min(jnp.where(logits == m1, lane, LANE), axis=-1, keepdims=True)
    rest = jnp.where(lane == i1, NEG_INF, logits)
    m2 = jnp.max(rest, axis=-1, keepdims=True)
    i2 = jnp.min(jnp.where(rest == m2, lane, LANE), axis=-1, keepdims=True)
    e2 = jnp.exp(m2 - m1)
    g1 = 1.0 / (1.0 + e2)
    g2 = e2 / (1.0 + e2)
    route_ref[...] = jnp.where(lane == 0, i1.astype(F32),
                               jnp.where(lane == 1, i2.astype(F32),
                                         jnp.where(lane == 2, g1, jnp.where(lane == 3, g2, 0.0))))


def _out_proj_route(x2, a, b, wa, wb, g, wr):
    T = x2.shape[0]
    tm = ROW_TILE
    row = lambda n: pl.BlockSpec((tm, n), lambda i: (i, 0))
    full = lambda w: pl.BlockSpec(w.shape, lambda i: (0, 0))
    return pl.pallas_call(
        _out_proj_route_body,
        grid=(T // tm,),
        in_specs=[row(D_MODEL), row(a.shape[1]), row(b.shape[1]), full(wa), full(wb), full(g), full(wr)],
        out_specs=[row(D_MODEL), row(LANE)],
        out_shape=[jax.ShapeDtypeStruct((T, D_MODEL), F32), jax.ShapeDtypeStruct((T, LANE), F32)],
        compiler_params=_cparams(("parallel",)),
        name="out_proj_router",
    )(x2, a, b, wa, wb, g, wr)


def _token_copy(src, dst, sem, src_tok, dst_tok):
    rows = lambda t: pl.ds(pl.multiple_of(t * TOKEN_ROWS, TOKEN_ROWS), TOKEN_ROWS)
    return pltpu.make_async_copy(src.at[rows(src_tok)], dst.at[rows(dst_tok)], sem)


def _dispatch_body(pos_ref, fill_ref, nu_ref, x_ref, o_hbm, tbuf, zero_scr, scat_sem, zero_sem):
    i = pl.program_id(0)
    nb = pl.num_programs(0)
    nt = x_ref.shape[0]
    tm = ROW_TILE

    def zero_tile(tok0):
        start = pl.multiple_of(tok0 * TOKEN_ROWS, tm * TOKEN_ROWS)
        fill = pltpu.make_async_copy(zero_scr, o_hbm.at[pl.ds(start, tm * TOKEN_ROWS)], zero_sem)
        fill.start()
        fill.wait()

    @pl.when(i == 0)
    def _():
        zero_scr[...] = jnp.zeros_like(zero_scr)
        for e in range(N_EXPERTS):
            @pl.when(fill_ref[e] >= 0)
            def _():
                zero_tile(fill_ref[e])

        def tail(t, carry):
            zero_tile(t * tm)
            return carry

        lax.fori_loop(nu_ref[0], o_hbm.shape[0] // (tm * TOKEN_ROWS), tail, 0)

    def wait_scatter(slot):
        for _ in range(TOP_K):
            pltpu.make_async_copy(tbuf.at[slot], o_hbm.at[pl.ds(0, nt * TOKEN_ROWS)],
                                  scat_sem.at[slot]).wait()

    slot = i % 3

    @pl.when(i >= 3)
    def _():
        wait_scatter(slot)

    for c in range(TOKEN_ROWS):
        _token_chunk(tbuf.at[slot], c, nt)[...] = x_ref[:, c * LANE:(c + 1) * LANE]
    base = i * nt * TOP_K

    def issue(r, carry):
        for k in range(TOP_K):
            _token_copy(tbuf.at[slot], o_hbm, scat_sem.at[slot], r, pos_ref[base + r * TOP_K + k]).start()
        return carry

    lax.fori_loop(0, nt, issue, 0, unroll=8)

    @pl.when(i == nb - 1)
    def _():
        for back in range(2, -1, -1):
            wait_scatter((i + 3 - back) % 3)


def _dispatch(pos, fill_rows, n_used, x2, n_rows_pad):
    nt = GATHER_ROWS
    assert x2.shape[0] // nt >= 3
    grid_spec = pltpu.PrefetchScalarGridSpec(
        num_scalar_prefetch=3,
        grid=(x2.shape[0] // nt,),
        in_specs=[pl.BlockSpec((nt, D_MODEL), lambda i, p, f, nu: (i, 0))],
        out_specs=pl.BlockSpec(memory_space=pl.ANY),
        scratch_shapes=[pltpu.VMEM((3, nt * TOKEN_ROWS, LANE), F32),
                        pltpu.VMEM((ROW_TILE * TOKEN_ROWS, LANE), F32),
                        pltpu.SemaphoreType.DMA((3,)), pltpu.SemaphoreType.DMA(())],
    )
    return pl.pallas_call(
        _dispatch_body,
        grid_spec=grid_spec,
        out_shape=jax.ShapeDtypeStruct((n_rows_pad * TOKEN_ROWS, LANE), F32),
        compiler_params=_cparams(("arbitrary",)),
        name="dispatch_rows",
    )(pos, fill_rows, n_used, x2)


def _combine_body(pos_ref, x_ref, route_ref, g_ref, ys_hbm, o_ref, buf, sem):
    i = pl.program_id(0)
    n = x_ref.shape[0]

    def fetch(block, slot):
        base = block * n

        def issue(r, carry):
            for k in range(TOP_K):
                _token_copy(ys_hbm, buf.at[slot, k], sem.at[slot],
                            pos_ref[(base + r) * TOP_K + k], r).start()
            return carry

        lax.fori_loop(0, n, issue, 0, unroll=8)

    slot = i % 2

    @pl.when(i == 0)
    def _():
        fetch(0, 0)

    @pl.when(i + 1 < pl.num_programs(0))
    def _():
        fetch(i + 1, 1 - slot)

    for k in range(TOP_K):
        pltpu.make_async_copy(ys_hbm.at[pl.ds(0, n * TOKEN_ROWS)], buf.at[slot, k], sem.at[slot]).wait()
    route = route_ref[...]
    gate = (route[:, 2:3], route[:, 3:4])
    sumsq = jnp.zeros((n, 1), F32)
    for c in range(TOKEN_ROWS):
        sl = slice(c * LANE, (c + 1) * LANE)
        y = x_ref[:, sl]
        for k in range(TOP_K):
            y = y + gate[k] * _token_chunk(buf.at[slot, k], c, n)[...]
        o_ref[:, sl] = y
        sumsq = sumsq + jnp.sum(y * y, axis=-1, keepdims=True)
    o_ref[...] = o_ref[...] * lax.rsqrt(sumsq * (1.0 / D_MODEL) + EPS) * g_ref[...]


def _combine(pos, x2, route, g, ys):
    T = x2.shape[0]
    n = GATHER_ROWS
    grid_spec = pltpu.PrefetchScalarGridSpec(
        num_scalar_prefetch=1,
        grid=(T // n,),
        in_specs=[pl.BlockSpec((n, D_MODEL), lambda i, p: (i, 0)),
                  pl.BlockSpec((n, LANE), lambda i, p: (i, 0)),
                  pl.BlockSpec((1, D_MODEL), lambda i, p: (0, 0)),
                  pl.BlockSpec(memory_space=pl.ANY)],
        out_specs=pl.BlockSpec((n, D_MODEL), lambda i, p: (i, 0)),
        scratch_shapes=[pltpu.VMEM((2, TOP_K, n * TOKEN_ROWS, LANE), F32),
                        pltpu.SemaphoreType.DMA((2,))],
    )
    return pl.pallas_call(
        _combine_body,
        grid_spec=grid_spec,
        out_shape=jax.ShapeDtypeStruct((T, D_MODEL), F32),
        compiler_params=_cparams(("arbitrary",)),
        name="combine_norm",
    )(pos, x2, route, g, ys)


def _even_weights(w_in, w_uq, w_ukv):
    n_main = w_in.shape[1] - B_ROPE
    w_kr = jnp.pad(w_in[:, n_main:], ((0, 0), (B_NOPE, LANE - B_NOPE - B_ROPE)))
    w_in_cat = jnp.concatenate([w_in[:, :n_main], w_kr], axis=1).astype(BF16)
    uq = w_uq.reshape(B_Q_LORA, B_HEADS, B_NOPE + B_ROPE)
    uq = jnp.pad(uq, ((0, 0), (0, 0), (0, LANE - B_NOPE - B_ROPE)))
    wuq = uq.reshape(B_Q_LORA, B_HEADS * LANE).astype(BF16)
    ukv = w_ukv.reshape(B_KV_LORA, B_HEADS, B_NOPE + B_V)
    wuk = ukv[..., :B_NOPE].reshape(B_KV_LORA, B_HEADS * B_NOPE).astype(BF16)
    wuv = ukv[..., B_NOPE:].reshape(B_KV_LORA, B_HEADS * B_V).astype(BF16)
    return w_in_cat, wuq, wuk, wuv


def _routing(route, n_rows_pad):
    tm = ROW_TILE
    e_flat = route[:, :TOP_K].astype(jnp.int32).reshape(-1)
    onehot = (e_flat[:, None] == jnp.arange(N_EXPERTS, dtype=jnp.int32)[None, :]).astype(jnp.int32)
    csum = jnp.cumsum(onehot, axis=0)
    rank = jnp.sum(csum * onehot, axis=1) - 1
    counts = csum[-1]
    tiles_e = (counts + tm - 1) // tm
    tiles_end = jnp.cumsum(tiles_e)
    row_off = (tiles_end - tiles_e) * tm
    pos = (jnp.sum(onehot * row_off[None, :], axis=1) + rank).astype(jnp.int32)
    n_tiles = n_rows_pad // tm
    tile_expert = jnp.sum(jnp.arange(n_tiles, dtype=jnp.int32)[:, None] >= tiles_end[None, :], axis=1)
    tile_expert = jnp.minimum(tile_expert, N_EXPERTS - 1).astype(jnp.int32)
    n_used = tiles_end[-1:].astype(jnp.int32)
    fill_rows = jnp.where(tiles_e > 0, (tiles_end - 1) * tm, -1).astype(jnp.int32)
    return pos, tile_expert, n_used, fill_rows


def kernel(x, positions, even_norm_mix, even_w_in, even_sinks, even_q_norm, even_w_uq, even_kv_norm, even_w_ukv, even_w_out, even_norm_ffn, even_w_gate, even_w_up, even_w_down, odd_norm_mix, odd_w_in, odd_ret_gn, odd_conv_w, odd_w_out, odd_norm_ffn, odd_router, odd_we_gate, odd_we_up, odd_we_down, final_norm):
    B, S, D = x.shape
    T = B * S
    tm = ROW_TILE
    x2 = x.reshape(T, D)
    pos2 = positions.reshape(T, 1)

    inv_freq = ROPE_THETA ** (-np.arange(0, B_ROPE, 2, dtype=np.float32) / B_ROPE)
    inv_row = np.zeros((1, LANE), np.float32)
    inv_row[0, B_NOPE:B_NOPE + B_ROPE // 2] = inv_freq
    inv_row[0, B_NOPE + B_ROPE // 2:B_NOPE + B_ROPE] = inv_freq
    slopes = 2.0 ** (-8.0 * (np.arange(A_HEADS, dtype=np.float32) + 1.0) / A_HEADS)
    slopes = jnp.asarray(slopes * LOG2E, F32)

    w_in_cat, wuq, wuk, wuv = _even_weights(even_w_in[0], even_w_uq[0], even_w_ukv[0])
    (qa, ka, valo, vahi, qb, kb, vb), (w_gate, w_up, w_down) = _even_in(
        x2, pos2, even_norm_mix[0][None], w_in_cat, jnp.asarray(inv_row),
        even_q_norm[0][None], wuq, even_kv_norm[0][None], wuk, wuv,
        [even_w_gate[0], even_w_up[0], even_w_down[0]])
    r3 = lambda t: t.reshape(B, S, t.shape[-1])
    ya = _swa(slopes, even_sinks[0] * LOG2E, r3(qa), r3(ka), r3(valo), r3(vahi)).reshape(T, -1)
    expert_w = (odd_we_gate[0], odd_we_up[0], odd_we_down[0])
    yb, expert_w_bf16 = _mla(r3(qb), r3(kb), r3(vb), [w.reshape(-1, w.shape[-1]) for w in expert_w])
    yb = yb.reshape(T, -1)
    we_gate, we_up, we_down = [wb.reshape(w.shape) for wb, w in zip(expert_w_bf16, expert_w)]
    w_out = even_w_out[0].astype(BF16)
    x2 = _out_proj_dense_ffn(x2, ya, yb, w_out[:ya.shape[1]], w_out[ya.shape[1]:],
                             even_norm_ffn[0][None], w_gate, w_up, w_down)

    z = _odd_in(x2, odd_norm_mix[0][None], odd_w_in[0].astype(BF16))
    yc, yd = _ret_conv(r3(z), odd_ret_gn[0][None], odd_conv_w[0])
    w_out = odd_w_out[0].astype(BF16)
    wr = jnp.pad(odd_router[0], ((0, 0), (0, LANE - N_EXPERTS)))
    wr_hi = wr.astype(BF16)
    wr_lo = (wr - wr_hi.astype(F32)).astype(BF16)
    x2, route = _out_proj_route(x2, yc.reshape(T, -1), yd.reshape(T, -1), w_out[:C_HEADS * C_VAL_DIM],
                                w_out[C_HEADS * C_VAL_DIM:], odd_norm_ffn[0][None],
                                jnp.concatenate([wr_hi, wr_lo], axis=1))
    n_rows_pad = T * TOP_K + N_EXPERTS * tm
    pos, tile_expert, n_used, fill_rows = _routing(route, n_rows_pad)
    xs = _dispatch(pos, fill_rows, n_used, x2, n_rows_pad)
    ys = _moe_ffn(tile_expert, n_used, xs, odd_norm_ffn[0][None], we_gate, we_up, we_down)
    out = _combine(pos, x2, route, final_norm[None], ys)
    return out.reshape(B, S, D)
```

```python
import math

import numpy as np
import jax
import jax.numpy as jnp
from jax import lax
from jax.experimental import pallas as pl
from jax.experimental.pallas import tpu as pltpu

F32 = jnp.float32
BF16 = jnp.bfloat16

D_MODEL = 1024
CHUNK = 64
A_HEADS = 8
A_KV_HEADS = 2
A_HEAD_DIM = 64
A_GROUP = A_HEADS // A_KV_HEADS
WINDOW_CHUNKS = 2
B_HEADS = 8
B_Q_LORA = 384
B_KV_LORA = 256
B_NOPE = 64
B_ROPE = 32
B_V = 64
ROPE_THETA = 10000.0
C_HEADS = 4
C_KEY_DIM = 128
C_VAL_DIM = 128
D_CH = 512
CONV_W = 3
D_FF = 3584
N_EXPERTS = 8
TOP_K = 2
EPS = 1e-6
NEG_INF = -1e30
LOG2E = math.log2(math.e)

LANE = 128
ROW_TILE = 512
FF_TILE = 1792
ATT_Q = 128
MLA_T = 256
RET_BLOCK = 256
GATHER_ROWS = 256
TOKEN_ROWS = D_MODEL // LANE
VMEM_LIMIT = 56 * 1024 * 1024

_EV_OFF = np.cumsum([0, A_HEADS * A_HEAD_DIM, A_KV_HEADS * A_HEAD_DIM, A_KV_HEADS * A_HEAD_DIM,
                     B_Q_LORA, B_KV_LORA, LANE])


def _cparams(semantics):
    return pltpu.CompilerParams(dimension_semantics=semantics, vmem_limit_bytes=VMEM_LIMIT)


def _rms(xf, g):
    return xf * lax.rsqrt(jnp.mean(xf * xf, axis=-1, keepdims=True) + EPS) * g


def _dot(a, b):
    return jnp.dot(a, b, preferred_element_type=F32)


def _dot_nt(a, b):
    return lax.dot_general(a, b, (((1,), (1,)), ((), ())), preferred_element_type=F32)


def _even_in_body(x_ref, pos_ref, g_ref, win_ref, invf_ref, qn_ref, wuq_ref, kvn_ref, wuk_ref, wuv_ref,
                  *rest):
    n_cast = (len(rest) - 7) // 2
    qa_ref, ka_ref, valo_ref, vahi_ref, qb_ref, kb_ref, vb_ref = rest[n_cast:n_cast + 7]
    for src, dst in zip(rest[:n_cast], rest[n_cast + 7:]):
        dst[...] = src[...].astype(BF16)
    tm = x_ref.shape[0]
    h = _rms(x_ref[...], g_ref[...]).astype(BF16)
    lane = lax.broadcasted_iota(jnp.int32, (tm, LANE), 1)
    lo = lane < LANE // 2

    def proj(k):
        return _dot(h, win_ref[:, int(_EV_OFF[k]):int(_EV_OFF[k + 1])])

    def swap_halves(t):
        return pltpu.roll(t, LANE // 2, axis=1)

    def split_pair(t, out_ref, first_tile):
        out_ref[:, first_tile * LANE:(first_tile + 1) * LANE] = jnp.where(lo, t, 0.0).astype(BF16)
        out_ref[:, (first_tile + 1) * LANE:(first_tile + 2) * LANE] = (
            jnp.where(lo, swap_halves(t), 0.0).astype(BF16))

    qa = proj(0) * (A_HEAD_DIM ** -0.5 * LOG2E)
    for pair in range(A_HEADS // 2):
        split_pair(qa[:, pair * LANE:(pair + 1) * LANE], qa_ref, 2 * pair)
    split_pair(proj(1), ka_ref, 0)
    va = proj(2)
    va_swapped = swap_halves(va)
    valo_ref[:, :LANE] = jnp.where(lo, va, 0.0).astype(BF16)
    valo_ref[:, LANE:] = jnp.where(lo, va_swapped, 0.0).astype(BF16)
    vahi_ref[:, :LANE] = jnp.where(lo, 0.0, va_swapped).astype(BF16)
    vahi_ref[:, LANE:] = jnp.where(lo, 0.0, va).astype(BF16)

    ang = pos_ref[...].astype(F32) * invf_ref[...]
    is_rope = (lane >= B_NOPE) & (lane < B_NOPE + B_ROPE)
    cosm = jnp.where(lane < B_NOPE, 1.0, jnp.where(is_rope, jnp.cos(ang), 0.0))
    sinm = jnp.where(is_rope, jnp.sin(ang), 0.0)
    first_half = lane < B_NOPE + B_ROPE // 2

    def rope(t):
        rot = jnp.where(first_half, -pltpu.roll(t, LANE - B_ROPE // 2, axis=1),
                        pltpu.roll(t, B_ROPE // 2, axis=1))
        return t * cosm + rot * sinm

    cq = _rms(proj(3), qn_ref[...]).astype(BF16)
    q_all = _dot(cq, wuq_ref[...])
    scale = (B_NOPE + B_ROPE) ** -0.5 * LOG2E
    for hd in range(B_HEADS):
        sl = slice(hd * LANE, (hd + 1) * LANE)
        qb_ref[:, sl] = (rope(q_all[:, sl]) * scale).astype(BF16)

    k_rope = rope(proj(5))
    ckv = _rms(proj(4), kvn_ref[...]).astype(BF16)
    k_all = _dot(ckv, wuk_ref[...])
    v_all = _dot(ckv, wuv_ref[...])
    ones_lo = jnp.where(lane == LANE // 2, 1.0, 0.0)
    ones_hi = jnp.where(lane == 0, 1.0, 0.0)
    for pair in range(B_HEADS // 2):
        sl = slice(pair * LANE, (pair + 1) * LANE)
        even = slice(2 * pair * LANE, (2 * pair + 1) * LANE)
        odd = slice((2 * pair + 1) * LANE, (2 * pair + 2) * LANE)
        kb_ref[:, even] = (jnp.where(lo, k_all[:, sl], 0.0) + k_rope).astype(BF16)
        kb_ref[:, odd] = (jnp.where(lo, swap_halves(k_all[:, sl]), 0.0) + k_rope).astype(BF16)
        vb_ref[:, even] = jnp.where(lo, v_all[:, sl], ones_lo).astype(BF16)
        vb_ref[:, odd] = jnp.where(lo, ones_hi, v_all[:, sl]).astype(BF16)


def _even_in(x2, pos2, g, w_in, inv_freq_row, q_norm, wuq, kv_norm, wuk, wuv, f32_weights):
    T = x2.shape[0]
    tm = ROW_TILE
    steps = T // tm
    row = lambda n: pl.BlockSpec((tm, n), lambda i: (i, 0))
    full = lambda a: pl.BlockSpec(a.shape, lambda i: (0, 0))
    slab = lambda w: pl.BlockSpec((w.shape[0] // steps, w.shape[1]), lambda i: (i, 0))
    outs = [A_HEADS * LANE, A_KV_HEADS * LANE, A_KV_HEADS * LANE, A_KV_HEADS * LANE,
            B_HEADS * LANE, B_HEADS * LANE, B_HEADS * LANE]
    res = pl.pallas_call(
        _even_in_body,
        grid=(steps,),
        in_specs=[row(D_MODEL), row(1), full(g), full(w_in), full(inv_freq_row), full(q_norm),
                  full(wuq), full(kv_norm), full(wuk), full(wuv)] + [slab(w) for w in f32_weights],
        out_specs=[row(n) for n in outs] + [slab(w) for w in f32_weights],
        out_shape=[jax.ShapeDtypeStruct((T, n), BF16) for n in outs]
        + [jax.ShapeDtypeStruct(w.shape, BF16) for w in f32_weights],
        compiler_params=_cparams(("parallel",)),
        name="even_in_proj",
    )(x2, pos2, g, w_in, inv_freq_row, q_norm, wuq, kv_norm, wuk, wuv, *f32_weights)
    return res[:7], res[7:]


def _swa_body(slopes_ref, sinks_ref, q_ref, k_ref, vlo_ref, vhi_ref, o_ref):
    hk = pl.program_id(1)
    S = q_ref.shape[1]
    win = ATT_Q + WINDOW_CHUNKS * CHUNK
    slope = [slopes_ref[hk * A_GROUP + g] for g in range(A_GROUP)]
    sink = [sinks_ref[hk * A_GROUP + g] for g in range(A_GROUP)]

    def mask_bias(delta):
        qpos = delta + lax.broadcasted_iota(jnp.int32, (ATT_Q, win), 0)
        kpos = lax.broadcasted_iota(jnp.int32, (ATT_Q, win), 1)
        qc = qpos // CHUNK
        kc = kpos // CHUNK
        valid = (kc <= qc) & (kc >= qc - WINDOW_CHUNKS)
        dist = jnp.abs(qpos - kpos).astype(F32)
        return [jnp.where(valid, -slope[g] * dist, NEG_INF) for g in range(A_GROUP)]

    bias_by_delta = {0: mask_bias(0), WINDOW_CHUNKS * CHUNK: mask_bias(WINDOW_CHUNKS * CHUNK)}
    for i in range(S // ATT_Q):
        q0 = i * ATT_Q
        k0 = max(q0 - WINDOW_CHUNKS * CHUNK, 0)
        bias = bias_by_delta[q0 - k0]
        kw = k_ref[0, k0:k0 + win, :]
        vw = (vlo_ref[0, k0:k0 + win, :], vhi_ref[0, k0:k0 + win, :])
        for pair in range(A_GROUP // 2):
            o = jnp.zeros((ATT_Q, LANE), F32)
            for w in range(2):
                g = pair * 2 + w
                s = _dot_nt(q_ref[0, q0:q0 + ATT_Q, g * LANE:(g + 1) * LANE], kw) + bias[g]
                m = jnp.maximum(jnp.max(s, axis=-1, keepdims=True), sink[g])
                p = jnp.exp2(s - m)
                den = jnp.sum(p, axis=-1, keepdims=True) + jnp.exp2(sink[g] - m)
                o = o + _dot(p.astype(BF16), vw[w]) * (1.0 / den)
            o_ref[0, q0:q0 + ATT_Q, pair * LANE:(pair + 1) * LANE] = o.astype(BF16)


def _swa(slopes, sinks, qa, ka, valo, vahi):
    B, S, _ = qa.shape
    smem = pl.BlockSpec(memory_space=pltpu.SMEM)
    kv = pl.BlockSpec((1, S, LANE), lambda b, h: (b, 0, h))
    return pl.pallas_call(
        _swa_body,
        grid=(B, A_KV_HEADS),
        in_specs=[smem, smem, pl.BlockSpec((1, S, A_GROUP * LANE), lambda b, h: (b, 0, h)), kv, kv, kv],
        out_specs=pl.BlockSpec((1, S, A_GROUP * A_HEAD_DIM), lambda b, h: (b, 0, h)),
        out_shape=jax.ShapeDtypeStruct((B, S, A_HEADS * A_HEAD_DIM), BF16),
        compiler_params=_cparams(("parallel", "parallel")),
        name="swa_attention",
    )(slopes, sinks, qa, ka, valo, vahi)


def _mla_body(q_ref, k_ref, v_ref, *rest):
    n_cast = (len(rest) - 1) // 2
    o_ref = rest[n_cast]
    for src, dst in zip(rest[:n_cast], rest[n_cast + 1:]):
        dst[...] = src[...].astype(BF16)
    S = q_ref.shape[1]
    T = MLA_T
    row = lax.broadcasted_iota(jnp.int32, (T, T), 0) // CHUNK
    col = lax.broadcasted_iota(jnp.int32, (T, T), 1) // CHUNK
    diag_ok = col <= row
    lo = lax.broadcasted_iota(jnp.int32, (T, LANE), 1) < LANE // 2

    for i in range(S // T):
        q0 = i * T
        normed = []
        for hh in range(2):
            sl = slice(hh * LANE, (hh + 1) * LANE)
            q = q_ref[0, q0:q0 + T, sl]
            s_diag = jnp.where(diag_ok, _dot_nt(q, k_ref[0, q0:q0 + T, sl]), NEG_INF)
            m = jnp.max(s_diag, axis=-1, keepdims=True)
            if i > 0:
                s_past = _dot_nt(q, k_ref[0, 0:q0, sl])
                m = jnp.maximum(m, jnp.max(s_past, axis=-1, keepdims=True))
            acc = _dot(jnp.exp2(s_diag - m).astype(BF16), v_ref[0, q0:q0 + T, sl])
            if i > 0:
                acc = acc + _dot(jnp.exp2(s_past - m).astype(BF16), v_ref[0, 0:q0, sl])
            den_lane = LANE // 2 if hh == 0 else 0
            normed.append(acc * (1.0 / acc[:, den_lane:den_lane + 1]))
        o_ref[0, q0:q0 + T, :] = jnp.where(lo, normed[0], normed[1]).astype(BF16)


def _mla(qb, kb, vb, f32_weights):
    B, S, _ = qb.shape
    n_pairs = B_HEADS // 2
    steps = B * n_pairs
    spec = pl.BlockSpec((1, S, 2 * LANE), lambda b, p: (b, 0, p))
    slab = lambda w: pl.BlockSpec((w.shape[0] // steps, w.shape[1]), lambda b, p: (b * n_pairs + p, 0))
    outs = pl.pallas_call(
        _mla_body,
        grid=(B, n_pairs),
        in_specs=[spec, spec, spec] + [slab(w) for w in f32_weights],
        out_specs=[pl.BlockSpec((1, S, 2 * B_V), lambda b, p: (b, 0, p))] + [slab(w) for w in f32_weights],
        out_shape=[jax.ShapeDtypeStruct((B, S, B_HEADS * B_V), BF16)]
        + [jax.ShapeDtypeStruct(w.shape, BF16) for w in f32_weights],
        compiler_params=_cparams(("parallel", "parallel")),
        name="mla_attention",
    )(qb, kb, vb, *f32_weights)
    return outs[0], outs[1:]


def _swiglu_partial(h, wg, wu, wd):
    gate = _dot(h, wg)
    up = _dot(h, wu)
    return _dot((gate * jax.nn.sigmoid(gate) * up).astype(BF16), wd)


def _dense_ffn_body(x_ref, a_ref, b_ref, wa_ref, wb_ref, g_ref, wg_ref, wu_ref, wd_ref, o_ref, h_scr):
    @pl.when(pl.program_id(1) == 0)
    def _():
        x1 = x_ref[...] + _dot(a_ref[...], wa_ref[...]) + _dot(b_ref[...], wb_ref[...])
        h_scr[...] = _rms(x1, g_ref[...]).astype(BF16)
        o_ref[...] = x1

    o_ref[...] += _swiglu_partial(h_scr[...], wg_ref[...], wu_ref[...], wd_ref[...])


def _out_proj_dense_ffn(x2, a, b, wa, wb, g, wg, wu, wd):
    T = x2.shape[0]
    tm, tf = ROW_TILE, FF_TILE
    row = lambda n: pl.BlockSpec((tm, n), lambda i, j: (i, 0))
    full = lambda w: pl.BlockSpec(w.shape, lambda i, j: (0, 0))
    return pl.pallas_call(
        _dense_ffn_body,
        grid=(T // tm, D_FF // tf),
        in_specs=[row(D_MODEL), row(a.shape[1]), row(b.shape[1]), full(wa), full(wb), full(g),
                  pl.BlockSpec((D_MODEL, tf), lambda i, j: (0, j)),
                  pl.BlockSpec((D_MODEL, tf), lambda i, j: (0, j)),
                  pl.BlockSpec((tf, D_MODEL), lambda i, j: (j, 0))],
        out_specs=row(D_MODEL),
        out_shape=jax.ShapeDtypeStruct((T, D_MODEL), F32),
        scratch_shapes=[pltpu.VMEM((tm, D_MODEL), BF16)],
        compiler_params=_cparams(("parallel", "arbitrary")),
        name="out_proj_swiglu",
    )(x2, a, b, wa, wb, g, wg, wu, wd)


def _token_chunk(ref, c, n):
    return ref.at[pl.ds(c, n, stride=TOKEN_ROWS), :]


def _moe_ffn_body(te_ref, nu_ref, x_ref, g_ref, wg_ref, wu_ref, wd_ref, o_ref, h_scr, acc_scr):
    i = pl.program_id(0)
    j = pl.program_id(1)
    nj = pl.num_programs(1)
    tm = h_scr.shape[0]

    @pl.when(i < nu_ref[0])
    def _():
        @pl.when(j == 0)
        def _():
            x = jnp.concatenate([_token_chunk(x_ref, c, tm)[...] for c in range(TOKEN_ROWS)], axis=1)
            h_scr[...] = _rms(x, g_ref[...]).astype(BF16)

        part = _swiglu_partial(h_scr[...], wg_ref[0], wu_ref[0], wd_ref[0])

        @pl.when(j == 0)
        def _():
            acc_scr[...] = part

        @pl.when((j > 0) & (j < nj - 1))
        def _():
            acc_scr[...] += part

        @pl.when(j == nj - 1)
        def _():
            res = acc_scr[...] + part
            for c in range(TOKEN_ROWS):
                _token_chunk(o_ref, c, tm)[...] = res[:, c * LANE:(c + 1) * LANE]

    @pl.when((i >= nu_ref[0]) & (j == 0))
    def _():
        o_ref[...] = jnp.zeros_like(o_ref)


def _moe_ffn(tile_expert, n_used, xs, g, wg, wu, wd):
    P = xs.shape[0] // TOKEN_ROWS
    tm, tf = ROW_TILE, FF_TILE
    nj = D_FF // tf
    assert nj >= 2

    def tile(i, nu):
        return jnp.minimum(i, nu[0] - 1)

    def ffcol(i, j, nu):
        return jnp.where(i < nu[0], j, nj - 1)

    grid_spec = pltpu.PrefetchScalarGridSpec(
        num_scalar_prefetch=2,
        grid=(P // tm, nj),
        in_specs=[
            pl.BlockSpec((tm * TOKEN_ROWS, LANE), lambda i, j, te, nu: (tile(i, nu), 0)),
            pl.BlockSpec((1, D_MODEL), lambda i, j, te, nu: (0, 0)),
            pl.BlockSpec((1, D_MODEL, tf), lambda i, j, te, nu: (te[tile(i, nu)], 0, ffcol(i, j, nu))),
            pl.BlockSpec((1, D_MODEL, tf), lambda i, j, te, nu: (te[tile(i, nu)], 0, ffcol(i, j, nu))),
            pl.BlockSpec((1, tf, D_MODEL), lambda i, j, te, nu: (te[tile(i, nu)], ffcol(i, j, nu), 0)),
        ],
        out_specs=pl.BlockSpec((tm * TOKEN_ROWS, LANE), lambda i, j, te, nu: (i, 0)),
        scratch_shapes=[pltpu.VMEM((tm, D_MODEL), BF16), pltpu.VMEM((tm, D_MODEL), F32)],
    )
    return pl.pallas_call(
        _moe_ffn_body,
        grid_spec=grid_spec,
        out_shape=jax.ShapeDtypeStruct((P * TOKEN_ROWS, LANE), F32),
        compiler_params=_cparams(("arbitrary", "arbitrary")),
        name="swiglu_experts",
    )(tile_expert, n_used, xs, g, wg, wu, wd)


def _odd_in_body(x_ref, g_ref, w_ref, o_ref):
    h = _rms(x_ref[...], g_ref[...]).astype(BF16)
    n = o_ref.shape[1]
    step = 512
    for c in range(n // step):
        sl = slice(c * step, (c + 1) * step)
        o_ref[:, sl] = _dot(h, w_ref[:, sl]).astype(BF16)


def _odd_in(x2, g, w):
    T = x2.shape[0]
    tm = ROW_TILE
    n = w.shape[1]
    return pl.pallas_call(
        _odd_in_body,
        grid=(T // tm,),
        in_specs=[pl.BlockSpec((tm, D_MODEL), lambda i: (i, 0)),
                  pl.BlockSpec(g.shape, lambda i: (0, 0)),
                  pl.BlockSpec(w.shape, lambda i: (0, 0))],
        out_specs=pl.BlockSpec((tm, n), lambda i: (i, 0)),
        out_shape=jax.ShapeDtypeStruct((T, n), BF16),
        compiler_params=_cparams(("parallel",)),
        name="odd_in_proj",
    )(x2, g, w)


def _ret_conv_body(q_ref, k_ref, v_ref, gt_ref, bd_ref, cd_ref, hd_ref, gn_ref, cw_ref,
                   yc_ref, yd_ref, u_scr):
    S = q_ref.shape[1]
    C = RET_BLOCK
    dk = C_KEY_DIM
    scale = dk ** -0.5
    ri = lax.broadcasted_iota(jnp.int32, (C, C), 0)
    ci = lax.broadcasted_iota(jnp.int32, (C, C), 1)
    diff = (ri - ci).astype(F32)
    pos = lax.broadcasted_iota(jnp.int32, (C, 1), 0).astype(F32)
    for hd in range(C_HEADS):
        log_g = math.log(1.0 - 2.0 ** (-5.0 - hd))
        intra = jnp.where(diff >= 0, jnp.exp(log_g * jnp.maximum(diff, 0.0)), 0.0) * scale
        cross = jnp.exp(log_g * (pos + 1.0))
        state = jnp.exp(log_g * (C - 1.0 - pos)) * scale
        block_decay = math.exp(log_g * C)
        sl = slice(hd * dk, (hd + 1) * dk)
        R = None
        for n in range(S // C):
            rows = slice(n * C, (n + 1) * C)
            q = q_ref[0, rows, sl]
            k = k_ref[0, rows, sl]
            v = v_ref[0, rows, sl]
            scores = _dot_nt(q, k) * intra
            y = _dot(scores.astype(BF16), v)
            kd_t = (k.astype(F32) * state).T.astype(BF16)
            kv = _dot(kd_t, v)
            if R is None:
                R = kv
            else:
                y = y + _dot(q, R.astype(BF16)) * cross
                R = block_decay * R + kv
            mu = jnp.mean(y, axis=-1, keepdims=True)
            yc = y - mu
            var = jnp.mean(yc * yc, axis=-1, keepdims=True)
            yn = yc * lax.rsqrt(var + EPS) * gn_ref[:, sl]
            gate = gt_ref[0, rows, sl].astype(F32)
            yc_ref[0, rows, sl] = (gate * jax.nn.sigmoid(gate) * yn).astype(BF16)

    halo = 8
    u_scr[0:halo, :] = jnp.zeros((halo, D_CH), F32)
    blk = 256
    for r in range(S // blk):
        rows = slice(r * blk, (r + 1) * blk)
        u_scr[halo + r * blk:halo + (r + 1) * blk, :] = (
            cd_ref[0, rows, :].astype(F32) * hd_ref[0, rows, :].astype(F32))
    for r in range(S // blk):
        acc = jnp.zeros((blk, D_CH), F32)
        for jw in range(CONV_W):
            lo = halo - (CONV_W - 1) + jw + r * blk
            acc = acc + cw_ref[jw:jw + 1, :] * u_scr[lo:lo + blk, :]
        rows = slice(r * blk, (r + 1) * blk)
        yd_ref[0, rows, :] = (bd_ref[0, rows, :].astype(F32) * acc).astype(BF16)


def _ret_conv(z, gn, conv_w):
    B, S, _ = z.shape
    col = lambda c: pl.BlockSpec((1, S, 512), lambda b: (b, 0, c))
    full = lambda a: pl.BlockSpec(a.shape, lambda b: (0, 0))
    out = pl.BlockSpec((1, S, 512), lambda b: (b, 0, 0))
    return pl.pallas_call(
        _ret_conv_body,
        grid=(B,),
        in_specs=[col(c) for c in range(7)] + [full(gn), full(conv_w)],
        out_specs=[out, out],
        out_shape=[jax.ShapeDtypeStruct((B, S, 512), BF16)] * 2,
        scratch_shapes=[pltpu.VMEM((S + 8, D_CH), F32)],
        compiler_params=_cparams(("parallel",)),
        name="retention_conv",
    )(z, z, z, z, z, z, z, gn, conv_w)


def _out_proj_route_body(x_ref, a_ref, b_ref, wa_ref, wb_ref, g_ref, wr_ref, o_ref, route_ref):
    x = x_ref[...] + _dot(a_ref[...], wa_ref[...]) + _dot(b_ref[...], wb_ref[...])
    o_ref[...] = x
    h = _rms(x, g_ref[...])
    h_hi = h.astype(BF16)
    h_lo = (h - h_hi.astype(F32)).astype(BF16)
    hi_pass = _dot(h_hi, wr_ref[...])
    logits = hi_pass[:, :LANE] + hi_pass[:, LANE:] + _dot(h_lo, wr_ref[:, :LANE])
    lane = lax.broadcasted_iota(jnp.int32, logits.shape, 1)
    logits = jnp.where(lane < N_EXPERTS, logits, NEG_INF)
    m1 = jnp.max(logits, axis=-1, keepdims=True)
    i1 = jnp.min(jnp.where(logits == m1, lane, LANE), axis=-1, keepdims=True)
    rest = jnp.where(lane == i1, NEG_INF, logits)
    m2 = jnp.max(rest, axis=-1, keepdims=True)
    i2 = jnp.min(jnp.where(rest == m2, lane, LANE), axis=-1, keepdims=True)
    e2 = jnp.exp(m2 - m1)
    g1 = 1.0 / (1.0 + e2)
    g2 = e2 / (1.0 + e2)
    route_ref[...] = jnp.where(lane == 0, i1.astype(F32),
                               jnp.where(lane == 1, i2.astype(F32),
                                         jnp.where(lane == 2, g1, jnp.where(lane == 3, g2, 0.0))))


def _out_proj_route(x2, a, b, wa, wb, g, wr):
    T = x2.shape[0]
    tm = ROW_TILE
    row = lambda n: pl.BlockSpec((tm, n), lambda i: (i, 0))
    full = lambda w: pl.BlockSpec(w.shape, lambda i: (0, 0))
    return pl.pallas_call(
        _out_proj_route_body,
        grid=(T // tm,),
        in_specs=[row(D_MODEL), row(a.shape[1]), row(b.shape[1]), full(wa), full(wb), full(g), full(wr)],
        out_specs=[row(D_MODEL), row(LANE)],
        out_shape=[jax.ShapeDtypeStruct((T, D_MODEL), F32), jax.ShapeDtypeStruct((T, LANE), F32)],
        compiler_params=_cparams(("parallel",)),
        name="out_proj_router",
    )(x2, a, b, wa, wb, g, wr)


def _token_copy(src, dst, sem, src_tok, dst_tok):
    rows = lambda t: pl.ds(pl.multiple_of(t * TOKEN_ROWS, TOKEN_ROWS), TOKEN_ROWS)
    return pltpu.make_async_copy(src.at[rows(src_tok)], dst.at[rows(dst_tok)], sem)


def _dispatch_body(pos_ref, fill_ref, nu_ref, x_ref, o_hbm, tbuf, zero_scr, scat_sem, zero_sem):
    i = pl.program_id(0)
    nb = pl.num_programs(0)
    nt = x_ref.shape[0]
    tm = ROW_TILE

    def zero_tile(tok0):
        start = pl.multiple_of(tok0 * TOKEN_ROWS, tm * TOKEN_ROWS)
        fill = pltpu.make_async_copy(zero_scr, o_hbm.at[pl.ds(start, tm * TOKEN_ROWS)], zero_sem)
        fill.start()
        fill.wait()

    @pl.when(i == 0)
    def _():
        zero_scr[...] = jnp.zeros_like(zero_scr)
        for e in range(N_EXPERTS):
            @pl.when(fill_ref[e] >= 0)
            def _():
                zero_tile(fill_ref[e])

        def tail(t, carry):
            zero_tile(t * tm)
            return carry

        lax.fori_loop(nu_ref[0], o_hbm.shape[0] // (tm * TOKEN_ROWS), tail, 0)

    def wait_scatter(slot):
        for _ in range(TOP_K):
            pltpu.make_async_copy(tbuf.at[slot], o_hbm.at[pl.ds(0, nt * TOKEN_ROWS)],
                                  scat_sem.at[slot]).wait()

    slot = i % 3

    @pl.when(i >= 3)
    def _():
        wait_scatter(slot)

    for c in range(TOKEN_ROWS):
        _token_chunk(tbuf.at[slot], c, nt)[...] = x_ref[:, c * LANE:(c + 1) * LANE]
    base = i * nt * TOP_K

    def issue(r, carry):
        for k in range(TOP_K):
            _token_copy(tbuf.at[slot], o_hbm, scat_sem.at[slot], r,
                        pos_ref[base + r * TOP_K + k]).start(priority=k % 2)
        return carry

    lax.fori_loop(0, nt, issue, 0, unroll=8)

    @pl.when(i == nb - 1)
    def _():
        for back in range(2, -1, -1):
            wait_scatter((i + 3 - back) % 3)


def _dispatch(pos, fill_rows, n_used, x2, n_rows_pad):
    nt = GATHER_ROWS
    assert x2.shape[0] // nt >= 3
    grid_spec = pltpu.PrefetchScalarGridSpec(
        num_scalar_prefetch=3,
        grid=(x2.shape[0] // nt,),
        in_specs=[pl.BlockSpec((nt, D_MODEL), lambda i, p, f, nu: (i, 0))],
        out_specs=pl.BlockSpec(memory_space=pl.ANY),
        scratch_shapes=[pltpu.VMEM((3, nt * TOKEN_ROWS, LANE), F32),
                        pltpu.VMEM((ROW_TILE * TOKEN_ROWS, LANE), F32),
                        pltpu.SemaphoreType.DMA((3,)), pltpu.SemaphoreType.DMA(())],
    )
    return pl.pallas_call(
        _dispatch_body,
        grid_spec=grid_spec,
        out_shape=jax.ShapeDtypeStruct((n_rows_pad * TOKEN_ROWS, LANE), F32),
        compiler_params=_cparams(("arbitrary",)),
        name="dispatch_rows",
    )(pos, fill_rows, n_used, x2)


def _combine_body(pos_ref, x_ref, route_ref, g_ref, ys_hbm, o_ref, buf, sem):
    i = pl.program_id(0)
    n = x_ref.shape[0]

    def fetch(block, slot):
        base = block * n

        def issue(r, carry):
            for k in range(TOP_K):
                _token_copy(ys_hbm, buf.at[slot, k], sem.at[slot],
                            pos_ref[(base + r) * TOP_K + k], r).start(priority=k % 2)
            return carry

        lax.fori_loop(0, n, issue, 0, unroll=8)

    slot = i % 2

    @pl.when(i == 0)
    def _():
        fetch(0, 0)

    @pl.when(i + 1 < pl.num_programs(0))
    def _():
        fetch(i + 1, 1 - slot)

    for k in range(TOP_K):
        pltpu.make_async_copy(ys_hbm.at[pl.ds(0, n * TOKEN_ROWS)], buf.at[slot, k], sem.at[slot]).wait()
    route = route_ref[...]
    gate = (route[:, 2:3], route[:, 3:4])
    sumsq = jnp.zeros((n, 1), F32)
    for c in range(TOKEN_ROWS):
        sl = slice(c * LANE, (c + 1) * LANE)
        y = x_ref[:, sl]
        for k in range(TOP_K):
            y = y + gate[k] * _token_chunk(buf.at[slot, k], c, n)[...]
        o_ref[:, sl] = y
        sumsq = sumsq + jnp.sum(y * y, axis=-1, keepdims=True)
    o_ref[...] = o_ref[...] * lax.rsqrt(sumsq * (1.0 / D_MODEL) + EPS) * g_ref[...]


def _combine(pos, x2, route, g, ys):
    T = x2.shape[0]
    n = GATHER_ROWS
    grid_spec = pltpu.PrefetchScalarGridSpec(
        num_scalar_prefetch=1,
        grid=(T // n,),
        in_specs=[pl.BlockSpec((n, D_MODEL), lambda i, p: (i, 0)),
                  pl.BlockSpec((n, LANE), lambda i, p: (i, 0)),
                  pl.BlockSpec((1, D_MODEL), lambda i, p: (0, 0)),
                  pl.BlockSpec(memory_space=pl.ANY)],
        out_specs=pl.BlockSpec((n, D_MODEL), lambda i, p: (i, 0)),
        scratch_shapes=[pltpu.VMEM((2, TOP_K, n * TOKEN_ROWS, LANE), F32),
                        pltpu.SemaphoreType.DMA((2,))],
    )
    return pl.pallas_call(
        _combine_body,
        grid_spec=grid_spec,
        out_shape=jax.ShapeDtypeStruct((T, D_MODEL), F32),
        compiler_params=_cparams(("arbitrary",)),
        name="combine_norm",
    )(pos, x2, route, g, ys)


def _even_weights(w_in, w_uq, w_ukv):
    n_main = w_in.shape[1] - B_ROPE
    w_kr = jnp.pad(w_in[:, n_main:], ((0, 0), (B_NOPE, LANE - B_NOPE - B_ROPE)))
    w_in_cat = jnp.concatenate([w_in[:, :n_main], w_kr], axis=1).astype(BF16)
    uq = w_uq.reshape(B_Q_LORA, B_HEADS, B_NOPE + B_ROPE)
    uq = jnp.pad(uq, ((0, 0), (0, 0), (0, LANE - B_NOPE - B_ROPE)))
    wuq = uq.reshape(B_Q_LORA, B_HEADS * LANE).astype(BF16)
    ukv = w_ukv.reshape(B_KV_LORA, B_HEADS, B_NOPE + B_V)
    wuk = ukv[..., :B_NOPE].reshape(B_KV_LORA, B_HEADS * B_NOPE).astype(BF16)
    wuv = ukv[..., B_NOPE:].reshape(B_KV_LORA, B_HEADS * B_V).astype(BF16)
    return w_in_cat, wuq, wuk, wuv


def _routing(route, n_rows_pad):
    tm = ROW_TILE
    e_flat = route[:, :TOP_K].astype(jnp.int32).reshape(-1)
    onehot = (e_flat[:, None] == jnp.arange(N_EXPERTS, dtype=jnp.int32)[None, :]).astype(jnp.int32)
    csum = jnp.cumsum(onehot, axis=0)
    rank = jnp.sum(csum * onehot, axis=1) - 1
    counts = csum[-1]
    tiles_e = (counts + tm - 1) // tm
    tiles_end = jnp.cumsum(tiles_e)
    row_off = (tiles_end - tiles_e) * tm
    pos = (jnp.sum(onehot * row_off[None, :], axis=1) + rank).astype(jnp.int32)
    n_tiles = n_rows_pad // tm
    tile_expert = jnp.sum(jnp.arange(n_tiles, dtype=jnp.int32)[:, None] >= tiles_end[None, :], axis=1)
    tile_expert = jnp.minimum(tile_expert, N_EXPERTS - 1).astype(jnp.int32)
    n_used = tiles_end[-1:].astype(jnp.int32)
    fill_rows = jnp.where(tiles_e > 0, (tiles_end - 1) * tm, -1).astype(jnp.int32)
    return pos, tile_expert, n_used, fill_rows


def kernel(x, positions, even_norm_mix, even_w_in, even_sinks, even_q_norm, even_w_uq, even_kv_norm, even_w_ukv, even_w_out, even_norm_ffn, even_w_gate, even_w_up, even_w_down, odd_norm_mix, odd_w_in, odd_ret_gn, odd_conv_w, odd_w_out, odd_norm_ffn, odd_router, odd_we_gate, odd_we_up, odd_we_down, final_norm):
    B, S, D = x.shape
    T = B * S
    tm = ROW_TILE
    x2 = x.reshape(T, D)
    pos2 = positions.reshape(T, 1)

    inv_freq = ROPE_THETA ** (-np.arange(0, B_ROPE, 2, dtype=np.float32) / B_ROPE)
    inv_row = np.zeros((1, LANE), np.float32)
    inv_row[0, B_NOPE:B_NOPE + B_ROPE // 2] = inv_freq
    inv_row[0, B_NOPE + B_ROPE // 2:B_NOPE + B_ROPE] = inv_freq
    slopes = 2.0 ** (-8.0 * (np.arange(A_HEADS, dtype=np.float32) + 1.0) / A_HEADS)
    slopes = jnp.asarray(slopes * LOG2E, F32)

    w_in_cat, wuq, wuk, wuv = _even_weights(even_w_in[0], even_w_uq[0], even_w_ukv[0])
    (qa, ka, valo, vahi, qb, kb, vb), (w_gate, w_up, w_down) = _even_in(
        x2, pos2, even_norm_mix[0][None], w_in_cat, jnp.asarray(inv_row),
        even_q_norm[0][None], wuq, even_kv_norm[0][None], wuk, wuv,
        [even_w_gate[0], even_w_up[0], even_w_down[0]])
    r3 = lambda t: t.reshape(B, S, t.shape[-1])
    ya = _swa(slopes, even_sinks[0] * LOG2E, r3(qa), r3(ka), r3(valo), r3(vahi)).reshape(T, -1)
    expert_w = (odd_we_gate[0], odd_we_up[0], odd_we_down[0])
    yb, expert_w_bf16 = _mla(r3(qb), r3(kb), r3(vb), [w.reshape(-1, w.shape[-1]) for w in expert_w])
    yb = yb.reshape(T, -1)
    we_gate, we_up, we_down = [wb.reshape(w.shape) for wb, w in zip(expert_w_bf16, expert_w)]
    w_out = even_w_out[0].astype(BF16)
    x2 = _out_proj_dense_ffn(x2, ya, yb, w_out[:ya.shape[1]], w_out[ya.shape[1]:],
                             even_norm_ffn[0][None], w_gate, w_up, w_down)

    z = _odd_in(x2, odd_norm_mix[0][None], odd_w_in[0].astype(BF16))
    yc, yd = _ret_conv(r3(z), odd_ret_gn[0][None], odd_conv_w[0])
    w_out = odd_w_out[0].astype(BF16)
    wr = jnp.pad(odd_router[0], ((0, 0), (0, LANE - N_EXPERTS)))
    wr_hi = wr.astype(BF16)
    wr_lo = (wr - wr_hi.astype(F32)).astype(BF16)
    x2, route = _out_proj_route(x2, yc.reshape(T, -1), yd.reshape(T, -1), w_out[:C_HEADS * C_VAL_DIM],
                                w_out[C_HEADS * C_VAL_DIM:], odd_norm_ffn[0][None],
                                jnp.concatenate([wr_hi, wr_lo], axis=1))
    n_rows_pad = T * TOP_K + N_EXPERTS * tm
    pos, tile_expert, n_used, fill_rows = _routing(route, n_rows_pad)
    xs = _dispatch(pos, fill_rows, n_used, x2, n_rows_pad)
    ys = _moe_ffn(tile_expert, n_used, xs, odd_norm_ffn[0][None], we_gate, we_up, we_down)
    out = _combine(pos, x2, route, final_norm[None], ys)
    return out.reshape(B, S, D)
```
